```python
import jax, jax.numpy as jnp
from jax import lax
import numpy as np

D_MODEL = 2048
BATCH = 8
SEQ = 4096
DEPTH = 4

N_MIXERS = 2
MEM_TOKENS = 256
BRANCH_WIDTH = 2 * D_MODEL
POOL_WINDOWS = (2, 4, 8, 16)
N_POOL_GROUPS = len(POOL_WINDOWS)
POOL_GROUP_WIDTH = BRANCH_WIDTH // N_POOL_GROUPS
DN_HEAD_DIM = 128
DN_V_HEADS = BRANCH_WIDTH // DN_HEAD_DIM
DN_QK_HEADS = DN_V_HEADS // 2
DN_KEY_WIDTH = DN_QK_HEADS * DN_HEAD_DIM
DN_CONV_WIDTH = 4
DN_CONV_CHANNELS = 2 * DN_KEY_WIDTH + BRANCH_WIDTH
DN_CHUNK = 64
XA_HEADS = 4
XA_WIDTH = D_MODEL
XA_HEAD_DIM = XA_WIDTH // XA_HEADS
MIX_WIDTH = BRANCH_WIDTH + XA_WIDTH
POOL_IN_WIDTH = BRANCH_WIDTH + MIX_WIDTH + XA_WIDTH
DELTA_IN_WIDTH = DN_CONV_CHANNELS + MIX_WIDTH + XA_WIDTH + 2 * DN_V_HEADS
N_POOL_LAYERS = (DEPTH + 1) // 2
N_DELTA_LAYERS = DEPTH // 2
EPS = 1e-6

kernel_name = "hybrid_pool_deltanet_memxattn_trunk"


def rms_norm(x, g):
    xf = x.astype(jnp.float32)
    y = xf * lax.rsqrt(jnp.mean(xf * xf, axis=-1, keepdims=True) + EPS)
    return (y * g.astype(jnp.float32)).astype(x.dtype)


def l2_normalize(x):
    return x * lax.rsqrt(jnp.sum(x * x, axis=-1, keepdims=True) + EPS)


def multiscale_pool_mixer(u, group_maps, scale):
    b, s, _ = u.shape
    uf = u.astype(jnp.float32).reshape(b, s, N_POOL_GROUPS, POOL_GROUP_WIDTH)
    csum = jnp.cumsum(uf, axis=1)
    pos = jnp.arange(1, s + 1, dtype=jnp.float32)
    diffs = []
    for gi, w in enumerate(POOL_WINDOWS):
        c = csum[:, :, gi]
        lagged = jnp.pad(c[:, : s - w], ((0, 0), (w, 0), (0, 0)))
        mean = (c - lagged) / jnp.minimum(pos, float(w))[None, :, None]
        diffs.append(mean - uf[:, :, gi])
    d = jnp.stack(diffs, axis=2).astype(u.dtype)
    y = jnp.einsum('bsgc,gcd->bsgd', d, group_maps).reshape(b, s, BRANCH_WIDTH)
    return y * scale


def causal_depthwise_conv(u, w):
    k = w.shape[0]
    return lax.conv_general_dilated(
        u, w[:, None, :].astype(u.dtype), window_strides=(1,), padding=[(k - 1, 0)],
        dimension_numbers=('NWC', 'WIO', 'NWC'), feature_group_count=u.shape[-1])


def chunk_gated_delta_rule(q, k, v, g, beta):
    b, s, h, dk = q.shape
    dv = v.shape[-1]
    c = DN_CHUNK
    n = s // c

    def to_chunks(t):
        return jnp.moveaxis(t.reshape((b, n, c, h) + t.shape[3:]), 3, 1)

    q = to_chunks(q) * (dk ** -0.5)
    k = to_chunks(k)
    v = to_chunks(v)
    beta = to_chunks(beta)
    g = jnp.cumsum(to_chunks(g), axis=-1)
    causal = jnp.tril(jnp.ones((c, c), dtype=bool))
    strict = jnp.tril(jnp.ones((c, c), dtype=bool), -1)
    diff = g[..., :, None] - g[..., None, :]
    decay = jnp.where(causal, jnp.exp(jnp.where(causal, diff, 0.0)), 0.0)
    k_beta = k * beta[..., None]
    lower = jnp.where(strict, jnp.einsum('bhncd,bhnmd->bhncm', k_beta, k) * decay, 0.0)
    eye = jnp.eye(c, dtype=jnp.float32)
    rhs = jnp.concatenate([v * beta[..., None], k_beta * jnp.exp(g)[..., None]], axis=-1)
    sol = lax.linalg.triangular_solve(eye + lower, rhs, left_side=True, lower=True, unit_diagonal=True)
    u_c, w_c = sol[..., :dv], sol[..., dv:]
    qk = jnp.where(causal, jnp.einsum('bhncd,bhnmd->bhncm', q, k) * decay, 0.0)
    g_last = g[..., -1]
    q_dec = q * jnp.exp(g)[..., None]
    k_dec = k * jnp.exp(g_last[..., None] - g)[..., None]

    def step(state, xs):
        q_i, k_i, qk_i, u_i, w_i, gl_i = xs
        v_new = u_i - jnp.einsum('bhcd,bhde->bhce', w_i, state)
        out = jnp.einsum('bhcd,bhde->bhce', q_i, state) + jnp.einsum('bhcm,bhme->bhce', qk_i, v_new)
        state = state * jnp.exp(gl_i)[..., None, None] + jnp.einsum('bhcd,bhce->bhde', k_i, v_new)
        return state, out

    xs = tuple(jnp.moveaxis(t, 2, 0) for t in (q_dec, k_dec, qk, u_c, w_c, g_last))
    state0 = jnp.zeros((b, h, dk, dv), jnp.float32)
    _, out = lax.scan(step, state0, xs)
    return jnp.transpose(out, (1, 0, 3, 2, 4)).reshape(b, s, h, dv)


def gated_deltanet_branch(qkv_in, b_logit, a_logit, conv_w, a_log, dt_bias, norm_g):
    bsz, s, _ = qkv_in.shape
    qkv = jax.nn.silu(causal_depthwise_conv(qkv_in, conv_w))
    q, k, v = jnp.split(qkv, [DN_KEY_WIDTH, 2 * DN_KEY_WIDTH], axis=-1)
    rep = DN_V_HEADS // DN_QK_HEADS
    q = jnp.repeat(l2_normalize(q.astype(jnp.float32).reshape(bsz, s, DN_QK_HEADS, DN_HEAD_DIM)), rep, axis=2)
    k = jnp.repeat(l2_normalize(k.astype(jnp.float32).reshape(bsz, s, DN_QK_HEADS, DN_HEAD_DIM)), rep, axis=2)
    v = v.astype(jnp.float32).reshape(bsz, s, DN_V_HEADS, DN_HEAD_DIM)
    beta = jax.nn.sigmoid(b_logit.astype(jnp.float32))
    g = -jnp.exp(a_log.astype(jnp.float32)) * jax.nn.softplus(a_logit.astype(jnp.float32) + dt_bias.astype(jnp.float32))
    o = chunk_gated_delta_rule(q, k, v, g, beta)
    o = rms_norm(o, norm_g)
    return o.reshape(bsz, s, BRANCH_WIDTH).astype(qkv_in.dtype)


def memory_cross_attention(q, mem_n, w_kv):
    b, s, _ = q.shape
    k, v = jnp.split(mem_n @ w_kv, 2, axis=-1)
    q = q.reshape(b, s, XA_HEADS, XA_HEAD_DIM)
    k = k.reshape(b, -1, XA_HEADS, XA_HEAD_DIM)
    v = v.reshape(b, -1, XA_HEADS, XA_HEAD_DIM)
    scores = jnp.einsum('bshd,bmhd->bhsm', q, k).astype(jnp.float32) * (XA_HEAD_DIM ** -0.5)
    p = jax.nn.softmax(scores, axis=-1).astype(v.dtype)
    return jnp.einsum('bhsm,bmhd->bshd', p, v).reshape(b, s, XA_WIDTH)


def setup_inputs(seed: int = 0) -> dict:
    key = jax.random.key(seed)
    ks = jax.random.split(key, 16)
    f32 = jnp.float32

    def normal(k, shape, scale):
        return jax.random.normal(k, shape, f32) * scale

    x = normal(ks[0], (BATCH, SEQ, D_MODEL), 1.0)
    mem = normal(ks[1], (BATCH, MEM_TOKENS, D_MODEL), 1.0)
    layer_norm_g = 1.0 + normal(ks[2], (DEPTH, D_MODEL), 0.02)
    mem_norm_g = 1.0 + normal(ks[3], (D_MODEL,), 0.02)
    final_norm_g = 1.0 + normal(ks[4], (D_MODEL,), 0.02)
    w_in_pool = normal(ks[5], (N_POOL_LAYERS, D_MODEL, POOL_IN_WIDTH), D_MODEL ** -0.5)
    pool_maps = normal(ks[6], (N_POOL_LAYERS, N_POOL_GROUPS, POOL_GROUP_WIDTH, POOL_GROUP_WIDTH), POOL_GROUP_WIDTH ** -0.5)
    pool_scale = 1.0 + normal(ks[7], (N_POOL_LAYERS, BRANCH_WIDTH), 0.02)
    w_in_delta = normal(ks[8], (N_DELTA_LAYERS, D_MODEL, DELTA_IN_WIDTH), D_MODEL ** -0.5)
    dn_conv_w = normal(ks[9], (N_DELTA_LAYERS, DN_CONV_WIDTH, DN_CONV_CHANNELS), DN_CONV_WIDTH ** -0.5)
    dn_a_log = jnp.log(jax.random.uniform(ks[10], (N_DELTA_LAYERS, DN_V_HEADS), f32, 1.0, 16.0))
    dt = jnp.exp(jax.random.uniform(ks[11], (N_DELTA_LAYERS, DN_V_HEADS), f32, np.log(1e-3), np.log(1e-1)))
    dn_dt_bias = dt + jnp.log(-jnp.expm1(-dt))
    dn_norm_g = 1.0 + normal(ks[12], (N_DELTA_LAYERS, DN_HEAD_DIM), 0.02)
    w_mem_kv = normal(ks[13], (DEPTH, D_MODEL, 2 * XA_WIDTH), D_MODEL ** -0.5)
    w_out = normal(ks[14], (DEPTH, MIX_WIDTH, D_MODEL), MIX_WIDTH ** -0.5)
    return {"x": x, "mem": mem, "layer_norm_g": layer_norm_g, "mem_norm_g": mem_norm_g,
            "final_norm_g": final_norm_g, "w_in_pool": w_in_pool, "pool_maps": pool_maps,
            "pool_scale": pool_scale, "w_in_delta": w_in_delta, "dn_conv_w": dn_conv_w,
            "dn_a_log": dn_a_log, "dn_dt_bias": dn_dt_bias, "dn_norm_g": dn_norm_g,
            "w_mem_kv": w_mem_kv, "w_out": w_out}


def reference(x, mem, layer_norm_g, mem_norm_g, final_norm_g, w_in_pool, pool_maps, pool_scale,
              w_in_delta, dn_conv_w, dn_a_log, dn_dt_bias, dn_norm_g, w_mem_kv, w_out):
    mem_n = rms_norm(mem, mem_norm_g)
    h = x
    d_z = DN_CONV_CHANNELS + MIX_WIDTH
    d_q = d_z + XA_WIDTH
    d_b = d_q + DN_V_HEADS
    for layer in range(DEPTH):
        j = layer // N_MIXERS
        xn = rms_norm(h, layer_norm_g[layer])
        if layer % N_MIXERS == 0:
            proj = xn @ w_in_pool[j]
            u, z, q_mem = jnp.split(proj, [BRANCH_WIDTH, BRANCH_WIDTH + MIX_WIDTH], axis=-1)
            branch = multiscale_pool_mixer(u, pool_maps[j], pool_scale[j])
        else:
            proj = xn @ w_in_delta[j]
            qkv_in, z, q_mem, b_logit, a_logit = jnp.split(proj, [DN_CONV_CHANNELS, d_z, d_q, d_b], axis=-1)
            branch = gated_deltanet_branch(qkv_in, b_logit, a_logit, dn_conv_w[j], dn_a_log[j],
                                           dn_dt_bias[j], dn_norm_g[j])
        mem_out = memory_cross_attention(q_mem, mem_n, w_mem_kv[layer])
        mixed = jnp.concatenate([branch, mem_out], axis=-1) * jax.nn.silu(z)
        h = h + mixed @ w_out[layer]
    return rms_norm(h, final_norm_g)
```

```python
import functools

import jax
import jax.numpy as jnp
from jax import lax
from jax.experimental import pallas as pl
from jax.experimental.pallas import tpu as pltpu

F32 = jnp.float32
BF16 = jnp.bfloat16
EPS = 1e-6

POOL_WINDOWS = (2, 4, 8, 16)
POOL_HALO = 32
XA_HEADS = 4
DN_HEAD_DIM = 128
DN_CONV_WIDTH = 4
CONV_HALO = 8
DELTA_CHUNK = 128
DELTA_HEADS_PER_ITER = 2
LANES = 128

V7X_VMEM_BYTES = 64 * 1024 * 1024


def _vmem_limit(estimate_bytes):
    return int(min(max(estimate_bytes * 5 // 4, 16 * 1024 * 1024), V7X_VMEM_BYTES - 6 * 1024 * 1024))


def _silu(x):
    return x * jax.nn.sigmoid(x)


def _rmsnorm_kernel(x_ref, g_ref, o_ref):
    x = x_ref[...].astype(F32)
    ms = jnp.mean(x * x, axis=-1, keepdims=True)
    o_ref[...] = (x * lax.rsqrt(ms + EPS) * g_ref[...]).astype(o_ref.dtype)


def _rmsnorm(x2d, g, out_dtype, tm=512):
    n, d = x2d.shape
    tm = min(tm, n)
    return pl.pallas_call(
        _rmsnorm_kernel,
        grid=(n // tm,),
        in_specs=[pl.BlockSpec((tm, d), lambda i: (i, 0)), pl.BlockSpec((1, d), lambda i: (0, 0))],
        out_specs=pl.BlockSpec((tm, d), lambda i: (i, 0)),
        out_shape=jax.ShapeDtypeStruct((n, d), out_dtype),
        compiler_params=pltpu.CompilerParams(dimension_semantics=("arbitrary",)),
        name="rmsnorm_cast",
    )(x2d, g.reshape(1, d).astype(F32))


def _mm_kernel(a_ref, w_ref, o_ref):
    o_ref[...] = jnp.dot(a_ref[...], w_ref[...], preferred_element_type=F32).astype(o_ref.dtype)


def _mm_side_kernel(a_ref, w_ref, ws_ref, o_ref, os_ref):
    o_ref[...] = jnp.dot(a_ref[...], w_ref[...], preferred_element_type=F32).astype(o_ref.dtype)

    @pl.when(pl.program_id(1) == 0)
    def _():
        os_ref[...] = jnp.dot(a_ref[...], ws_ref[...], preferred_element_type=F32)


def _matmul(a, w, out_dtype, side_w=None, tm=1024, tn=1024, name="proj"):
    m, k = a.shape
    n = w.shape[1]
    tm, tn = min(tm, m), min(tn, n)
    grid = (m // tm, n // tn)
    a_spec = pl.BlockSpec((tm, k), lambda i, j: (i, 0))
    w_spec = pl.BlockSpec((k, tn), lambda i, j: (0, j))
    o_spec = pl.BlockSpec((tm, tn), lambda i, j: (i, j))
    est = 2 * (tm * k * 2 + k * tn * 2 + tm * tn * 2) + tm * tn * 4
    params = pltpu.CompilerParams(dimension_semantics=("arbitrary", "arbitrary"),
                                  vmem_limit_bytes=_vmem_limit(est + 4 * 1024 * 1024))
    if side_w is None:
        return pl.pallas_call(
            _mm_kernel, grid=grid, in_specs=[a_spec, w_spec], out_specs=o_spec,
            out_shape=jax.ShapeDtypeStruct((m, n), out_dtype), compiler_params=params, name=name,
        )(a, w)
    ns = side_w.shape[1]
    return pl.pallas_call(
        _mm_side_kernel, grid=grid,
        in_specs=[a_spec, w_spec, pl.BlockSpec((k, ns), lambda i, j: (0, 0))],
        out_specs=[o_spec, pl.BlockSpec((tm, ns), lambda i, j: (i, 0))],
        out_shape=[jax.ShapeDtypeStruct((m, n), out_dtype), jax.ShapeDtypeStruct((m, ns), F32)],
        compiler_params=params, name=name,
    )(a, w, side_w)


def _pool_kernel(u_ref, z_ref, maps_ref, scale_ref, o_ref, buf_a, buf_b, halo, *, tm, tiles_per_seq, cg):
    t = pl.program_id(0) % tiles_per_seq
    hl = POOL_HALO
    ext = hl + tm
    pos = (t * tm + 1 + lax.broadcasted_iota(jnp.int32, (tm, 1), 0)).astype(F32)

    @pl.when(t == 0)
    def _():
        halo[...] = jnp.zeros_like(halo)

    for g, w in enumerate(POOL_WINDOWS):
        cols = slice(g * cg, (g + 1) * cg)
        u = u_ref[:, cols].astype(F32)
        buf_a[0:hl, :] = halo[:, cols]
        buf_a[hl:ext, :] = u
        halo[:, cols] = u[tm - hl:, :]
        src, dst, lo, shift = buf_a, buf_b, 0, 1
        while shift < w:
            lo += 8
            dst[lo:ext, :] = src[lo:ext, :] + src[lo - shift:ext - shift, :]
            src, dst, shift = dst, src, shift * 2
        inv_cnt = 1.0 / jnp.minimum(pos, float(w))
        d = (src[hl:ext, :] * inv_cnt - u).astype(BF16)
        y = jnp.dot(d, maps_ref[g], preferred_element_type=F32)
        o_ref[:, cols] = (y * scale_ref[:, cols] * _silu(z_ref[:, cols].astype(F32))).astype(o_ref.dtype)


def _pool_mix(proj, maps, scale, seq, bw, tm=256):
    n = proj.shape[0]
    tm = min(tm, seq)
    cg = bw // len(POOL_WINDOWS)
    kern = functools.partial(_pool_kernel, tm=tm, tiles_per_seq=seq // tm, cg=cg)
    est = 2 * (3 * tm * bw * 2 + maps.size * 2) + 2 * (POOL_HALO + tm) * cg * 4 + POOL_HALO * bw * 4 + 4 * tm * cg * 4
    return pl.pallas_call(
        kern,
        grid=(n // tm,),
        in_specs=[
            pl.BlockSpec((tm, bw), lambda i: (i, 0)),
            pl.BlockSpec((tm, bw), lambda i: (i, 1)),
            pl.BlockSpec(maps.shape, lambda i: (0, 0, 0)),
            pl.BlockSpec((1, bw), lambda i: (0, 0)),
        ],
        out_specs=pl.BlockSpec((tm, bw), lambda i: (i, 0)),
        out_shape=jax.ShapeDtypeStruct((n, bw), BF16),
        scratch_shapes=[
            pltpu.VMEM((POOL_HALO + tm, cg), F32),
            pltpu.VMEM((POOL_HALO + tm, cg), F32),
            pltpu.VMEM((POOL_HALO, bw), F32),
        ],
        compiler_params=pltpu.CompilerParams(dimension_semantics=("arbitrary",), vmem_limit_bytes=_vmem_limit(est)),
        name="pool_mix",
    )(proj, proj, maps, scale.reshape(1, bw).astype(F32))


def _xattn_kernel(q_ref, z_ref, k_ref, v_ref, o_ref, *, hd):
    scale = hd ** -0.5
    for h in range(XA_HEADS):
        cols = slice(h * hd, (h + 1) * hd)
        s = lax.dot_general(q_ref[:, cols], k_ref[:, cols], (((1,), (1,)), ((), ())),
                            preferred_element_type=F32) * scale
        e = jnp.exp(s - jnp.max(s, axis=-1, keepdims=True))
        p = e / jnp.sum(e, axis=-1, keepdims=True)
        o = jnp.dot(p.astype(BF16), v_ref[:, cols], preferred_element_type=F32)
        o_ref[:, cols] = (o * _silu(z_ref[:, cols].astype(F32))).astype(o_ref.dtype)


def _xattn(proj, kv, layer, seq, mem_tokens, q_blk, z_blk, xw, tm=512):
    n = proj.shape[0]
    tm = min(tm, seq)
    tps = seq // tm
    kern = functools.partial(_xattn_kernel, hd=xw // XA_HEADS)
    est = 2 * (3 * tm * xw * 2 + 2 * mem_tokens * xw * 2) + 6 * tm * mem_tokens * 4
    return pl.pallas_call(
        kern,
        grid=(n // tm,),
        in_specs=[
            pl.BlockSpec((tm, xw), lambda i: (i, q_blk)),
            pl.BlockSpec((tm, xw), lambda i: (i, z_blk)),
            pl.BlockSpec((mem_tokens, xw), lambda i: (i // tps, 2 * layer)),
            pl.BlockSpec((mem_tokens, xw), lambda i: (i // tps, 2 * layer + 1)),
        ],
        out_specs=pl.BlockSpec((tm, xw), lambda i: (i, 0)),
        out_shape=jax.ShapeDtypeStruct((n, xw), BF16),
        compiler_params=pltpu.CompilerParams(dimension_semantics=("arbitrary",), vmem_limit_bytes=_vmem_limit(est)),
        name="mem_xattn",
    )(proj, proj, kv, kv)


def _out_kernel(br_ref, mo_ref, w_ref, h_ref, g_ref, *out_refs, bw, final):
    acc = jnp.dot(br_ref[...], w_ref[0:bw, :], preferred_element_type=F32)
    acc = acc + jnp.dot(mo_ref[...], w_ref[bw:, :], preferred_element_type=F32)
    hn = h_ref[...] + acc
    y = hn * lax.rsqrt(jnp.mean(hn * hn, axis=-1, keepdims=True) + EPS) * g_ref[...]
    if final:
        out_refs[0][...] = y
    else:
        out_refs[0][...] = hn
        out_refs[1][...] = y.astype(out_refs[1].dtype)


def _out_proj(branch, memo, w_out, h, g_next, final, tm=256):
    n, bw = branch.shape
    xw = memo.shape[1]
    d = w_out.shape[1]
    tm = min(tm, n)
    kern = functools.partial(_out_kernel, bw=bw, final=final)
    row = lambda i: (i, 0)
    if final:
        out_specs = [pl.BlockSpec((tm, d), row)]
        out_shape = [jax.ShapeDtypeStruct((n, d), F32)]
    else:
        out_specs = [pl.BlockSpec((tm, d), row), pl.BlockSpec((tm, d), row)]
        out_shape = [jax.ShapeDtypeStruct((n, d), F32), jax.ShapeDtypeStruct((n, d), BF16)]
    est = w_out.size * 2 + 2 * (tm * (bw + xw) * 2 + tm * d * 4 * 2 + tm * d * 2) + 3 * tm * d * 4
    return pl.pallas_call(
        kern,
        grid=(n // tm,),
        in_specs=[
            pl.BlockSpec((tm, bw), row),
            pl.BlockSpec((tm, xw), row),
            pl.BlockSpec(w_out.shape, lambda i: (0, 0), pipeline_mode=pl.Buffered(1)),
            pl.BlockSpec((tm, d), row),
            pl.BlockSpec((1, d), lambda i: (0, 0)),
        ],
        out_specs=out_specs,
        out_shape=out_shape,
        compiler_params=pltpu.CompilerParams(dimension_semantics=("arbitrary",), vmem_limit_bytes=_vmem_limit(est)),
        name="out_proj",
    )(branch, memo, w_out, h, g_next.reshape(1, d).astype(F32))


def _bdot(a, b):
    return jnp.dot(a.astype(BF16), b.astype(BF16), preferred_element_type=F32)


def _unit_lower_inverse(a, msk_ref, eye):
    n1 = -(a * msk_ref[0])
    t = eye + n1
    p = n1
    for _ in range(3):
        p = _bdot(p, p)
        t = t + _bdot(t, p)
    for lvl in range(1, msk_ref.shape[0]):
        t = t - _bdot(t, _bdot(a * msk_ref[lvl], t))
    return t


def _delta_kernel(qkv_ref, z_ref, ba_ref, cw_ref, alog_ref, dtb_ref, ng_ref, o_ref,
                  halo, ext, state, qk_s, rhs_s, qe_s, kdt_s, dma_s, dq_s, egl_s, msk_s, out_s,
                  *, c, n_qk, strip):
    dk = DN_HEAD_DIM
    rep = 2
    n_v = n_qk * rep
    kw = n_qk * dk
    row = lax.broadcasted_iota(jnp.int32, (c, c), 0)
    col = lax.broadcasted_iota(jnp.int32, (c, c), 1)
    causal = row >= col
    strict = row > col
    eye = (row == col).astype(F32)

    @pl.when(pl.program_id(1) == 0)
    def _():
        halo[...] = jnp.zeros_like(halo)
        state[...] = jnp.zeros_like(state)
        prev = None
        for lvl, sh in enumerate((4, 5, 6, 7)):
            same = ((row >> sh) == (col >> sh)).astype(F32)
            msk_s[lvl] = same if prev is None else same - prev
            prev = same

    beta = jax.nn.sigmoid(ba_ref[:, 0:LANES])
    a_in = ba_ref[:, LANES:2 * LANES] + dtb_ref[...]
    softplus = jnp.maximum(a_in, 0.0) + jnp.log1p(jnp.exp(-jnp.abs(a_in)))
    g = -jnp.exp(alog_ref[...]) * softplus
    gc = jnp.dot(causal.astype(F32), g, precision=lax.Precision.HIGHEST, preferred_element_type=F32)
    eg = jnp.exp(gc)
    egl = jnp.exp(gc[c - 1:c, :] - gc)
    gct = gc.T
    egl_s[...] = jnp.exp(jnp.broadcast_to(gct[:, c - 1:c], (LANES, LANES)))
    bscale = beta * eg
    for hv in range(n_v):
        diff = gc[:, hv:hv + 1] - gct[hv:hv + 1, :]
        dm = jnp.where(causal, jnp.exp(jnp.where(causal, diff, 0.0)), 0.0)
        dq_s[hv] = dm
        dma_s[hv] = jnp.where(strict, dm * beta[:, hv:hv + 1], 0.0)

    for s0 in range(0, qkv_ref.shape[1], strip):
        cs = slice(s0, s0 + strip)
        x = qkv_ref[:, cs].astype(F32)
        ext[0:CONV_HALO, :] = halo[:, cs]
        ext[CONV_HALO:CONV_HALO + c, :] = x
        halo[:, cs] = x[c - CONV_HALO:, :]
        y = x * cw_ref[DN_CONV_WIDTH - 1:DN_CONV_WIDTH, cs]
        for j in range(1, DN_CONV_WIDTH):
            y = y + ext[CONV_HALO - j:CONV_HALO - j + c, :] * cw_ref[DN_CONV_WIDTH - 1 - j:DN_CONV_WIDTH - j, cs]
        act = _silu(y)
        for h0 in range(0, strip, dk):
            a = act[:, h0:h0 + dk]
            cidx = s0 + h0
            if cidx < 2 * kw:
                a = a * lax.rsqrt(jnp.sum(a * a, axis=-1, keepdims=True) + EPS)
            if cidx < kw:
                j = cidx // dk
                a = a * (dk ** -0.5)
                qk_s[j, 0:c, :] = a.astype(BF16)
                for r in range(rep):
                    hv = rep * j + r
                    qe_s[hv] = (a * eg[:, hv:hv + 1]).astype(BF16)
            elif cidx < 2 * kw:
                j = (cidx - kw) // dk
                qk_s[j, c:2 * c, :] = a.astype(BF16)
                for r in range(rep):
                    hv = rep * j + r
                    rhs_s[hv, :, dk:2 * dk] = (a * bscale[:, hv:hv + 1]).astype(BF16)
                    kdt_s[hv] = (a * egl[:, hv:hv + 1]).T.astype(BF16)
            else:
                hv = (cidx - 2 * kw) // dk
                rhs_s[hv, :, 0:dk] = (a * beta[:, hv:hv + 1]).astype(BF16)

    ng = ng_ref[...]

    def head_group(jb, carry):
        for jj in range(DELTA_HEADS_PER_ITER):
            j = jb * DELTA_HEADS_PER_ITER + jj
            p = lax.dot_general(qk_s[j], qk_s[j, pl.ds(c, c), :], (((1,), (1,)), ((), ())),
                                preferred_element_type=F32)
            for r in range(rep):
                hv = rep * j + r
                t = _unit_lower_inverse(p[c:2 * c] * dma_s[hv], msk_s, eye)
                uw = jnp.dot(t.astype(BF16), rhs_s[hv], preferred_element_type=F32)
                st = state[hv]
                wq = jnp.dot(jnp.concatenate([uw[:, dk:2 * dk].astype(BF16), qe_s[hv]], axis=0),
                             st.astype(BF16), preferred_element_type=F32)
                v_new = (uw[:, 0:dk] - wq[0:c]).astype(BF16)
                o = wq[c:2 * c] + jnp.dot((p[0:c] * dq_s[hv]).astype(BF16), v_new, preferred_element_type=F32)
                state[hv] = st * egl_s[pl.ds(hv, 1), :] + jnp.dot(kdt_s[hv], v_new, preferred_element_type=F32)
                out_s[hv] = o * lax.rsqrt(jnp.mean(o * o, axis=-1, keepdims=True) + EPS) * ng
        return carry

    lax.fori_loop(0, n_qk // DELTA_HEADS_PER_ITER, head_group, 0)

    for hv in range(n_v):
        cols = slice(hv * dk, (hv + 1) * dk)
        o_ref[:, cols] = (out_s[hv] * _silu(z_ref[:, cols].astype(F32))).astype(o_ref.dtype)


def _delta_mix(proj, ba, conv_w, a_log, dt_bias, norm_g, batch, seq, bw):
    n = proj.shape[0]
    c = min(DELTA_CHUNK, seq)
    dk = DN_HEAD_DIM
    n_v = bw // dk
    n_qk = n_v // 2
    cw = 2 * n_qk * dk + bw
    nch = seq // c
    strip = 512
    kern = functools.partial(_delta_kernel, c=c, n_qk=n_qk, strip=strip)

    def pad_lanes(v):
        return jnp.zeros((1, LANES), F32).at[0, :v.shape[0]].set(v.astype(F32))

    rowblk = lambda b, t: (b * nch + t, 0)
    est = (2 * (c * cw * 2 + 2 * c * bw * 2 + c * 2 * LANES * 4) + CONV_HALO * cw * 4 + n_v * dk * dk * 4
           + n_qk * 2 * c * dk * 2 + n_v * c * 2 * dk * 2 + 2 * n_v * c * dk * 2 + 2 * n_v * c * c * 4
           + n_v * c * dk * 4 + 8 * c * c * 4 + 8 * 1024 * 1024)
    return pl.pallas_call(
        kern,
        grid=(batch, nch),
        in_specs=[
            pl.BlockSpec((c, cw), rowblk),
            pl.BlockSpec((c, bw), lambda b, t: (b * nch + t, cw // bw)),
            pl.BlockSpec((c, 2 * LANES), rowblk),
            pl.BlockSpec((DN_CONV_WIDTH, cw), lambda b, t: (0, 0)),
            pl.BlockSpec((1, LANES), lambda b, t: (0, 0)),
            pl.BlockSpec((1, LANES), lambda b, t: (0, 0)),
            pl.BlockSpec((1, dk), lambda b, t: (0, 0)),
        ],
        out_specs=pl.BlockSpec((c, bw), rowblk),
        out_shape=jax.ShapeDtypeStruct((n, bw), BF16),
        scratch_shapes=[
            pltpu.VMEM((CONV_HALO, cw), F32),
            pltpu.VMEM((CONV_HALO + c, strip), F32),
            pltpu.VMEM((n_v, dk, dk), F32),
            pltpu.VMEM((n_qk, 2 * c, dk), BF16),
            pltpu.VMEM((n_v, c, 2 * dk), BF16),
            pltpu.VMEM((n_v, c, dk), BF16),
            pltpu.VMEM((n_v, dk, c), BF16),
            pltpu.VMEM((n_v, c, c), F32),
            pltpu.VMEM((n_v, c, c), F32),
            pltpu.VMEM((LANES, LANES), F32),
            pltpu.VMEM((4, c, c), F32),
            pltpu.VMEM((n_v, c, dk), F32),
        ],
        compiler_params=pltpu.CompilerParams(dimension_semantics=("arbitrary", "arbitrary"),
                                             vmem_limit_bytes=_vmem_limit(est)),
        name="delta_mix",
    )(proj, proj, ba, conv_w.astype(F32), pad_lanes(a_log), pad_lanes(dt_bias),
      norm_g.reshape(1, dk).astype(F32))


def kernel(x, mem, layer_norm_g, mem_norm_g, final_norm_g, w_in_pool, pool_maps, pool_scale, w_in_delta,
           dn_conv_w, dn_a_log, dn_dt_bias, dn_norm_g, w_mem_kv, w_out):
    batch, seq, d = x.shape
    mem_tokens = mem.shape[1]
    depth = layer_norm_g.shape[0]
    n = batch * seq
    bw = pool_scale.shape[1]
    xw = w_mem_kv.shape[2] // 2
    n_v = dn_a_log.shape[1]
    main_w = w_in_delta.shape[2] - 2 * n_v

    mem_n = _rmsnorm(mem.reshape(batch * mem_tokens, d), mem_norm_g, BF16)
    w_kv_all = jnp.transpose(w_mem_kv, (1, 0, 2)).reshape(d, depth * 2 * xw).astype(BF16)
    kv = _matmul(mem_n, w_kv_all, BF16, name="mem_kv")

    h = x.reshape(n, d)
    xn = _rmsnorm(h, layer_norm_g[0], BF16)
    for layer in range(depth):
        j = layer // 2
        if layer % 2 == 0:
            proj = _matmul(xn, w_in_pool[j].astype(BF16), BF16, name="in_proj_pool")
            branch = _pool_mix(proj, pool_maps[j].astype(BF16), pool_scale[j], seq, bw)
            z_mem_blk, q_blk = (bw + bw) // xw, (bw + bw + xw) // xw
        else:
            w = w_in_delta[j]
            side = jnp.zeros((d, 2 * LANES), BF16)
            side = side.at[:, 0:n_v].set(w[:, main_w:main_w + n_v].astype(BF16))
            side = side.at[:, LANES:LANES + n_v].set(w[:, main_w + n_v:].astype(BF16))
            proj, ba = _matmul(xn, w[:, :main_w].astype(BF16), BF16, side_w=side, name="in_proj_delta")
            branch = _delta_mix(proj, ba, dn_conv_w[j], dn_a_log[j], dn_dt_bias[j], dn_norm_g[j], batch, seq, bw)
            cwid = main_w - (bw + xw) - xw
            z_mem_blk, q_blk = (cwid + bw) // xw, (cwid + bw + xw) // xw
        memo = _xattn(proj, kv, layer, seq, mem_tokens, q_blk, z_mem_blk, xw)
        final = layer == depth - 1
        g_next = final_norm_g if final else layer_norm_g[layer + 1]
        outs = _out_proj(branch, memo, w_out[layer].astype(BF16), h, g_next, final)
        if final:
            return outs[0].reshape(batch, seq, d)
        h, xn = outs
```

```python
import functools

import jax
import jax.numpy as jnp
from jax import lax
from jax.experimental import pallas as pl
from jax.experimental.pallas import tpu as pltpu

F32 = jnp.float32
BF16 = jnp.bfloat16
EPS = 1e-6

POOL_WINDOWS = (2, 4, 8, 16)
POOL_HALO = 32
XA_HEADS = 4
DN_HEAD_DIM = 128
DN_CONV_WIDTH = 4
CONV_HALO = 8
DELTA_CHUNK = 128
DELTA_GROUP_QK = 8
LANES = 128

V7X_VMEM_BYTES = 64 * 1024 * 1024


def _vmem_limit(estimate_bytes):
    return int(min(max(estimate_bytes * 5 // 4, 16 * 1024 * 1024), V7X_VMEM_BYTES - 6 * 1024 * 1024))


def _silu(x):
    return x * jax.nn.sigmoid(x)


def _rmsnorm_kernel(x_ref, g_ref, o_ref):
    x = x_ref[...].astype(F32)
    ms = jnp.mean(x * x, axis=-1, keepdims=True)
    o_ref[...] = (x * lax.rsqrt(ms + EPS) * g_ref[...]).astype(o_ref.dtype)


def _rmsnorm(x2d, g, out_dtype, tm=512):
    n, d = x2d.shape
    tm = min(tm, n)
    return pl.pallas_call(
        _rmsnorm_kernel,
        grid=(n // tm,),
        in_specs=[pl.BlockSpec((tm, d), lambda i: (i, 0)), pl.BlockSpec((1, d), lambda i: (0, 0))],
        out_specs=pl.BlockSpec((tm, d), lambda i: (i, 0)),
        out_shape=jax.ShapeDtypeStruct((n, d), out_dtype),
        compiler_params=pltpu.CompilerParams(dimension_semantics=("arbitrary",)),
        name="rmsnorm_cast",
    )(x2d, g.reshape(1, d).astype(F32))


def _mm_kernel(a_ref, w_ref, o_ref):
    o_ref[...] = jnp.dot(a_ref[...], w_ref[...], preferred_element_type=F32).astype(o_ref.dtype)


def _mm_side_kernel(a_ref, w_ref, ws_ref, o_ref, os_ref):
    o_ref[...] = jnp.dot(a_ref[...], w_ref[...], preferred_element_type=F32).astype(o_ref.dtype)

    @pl.when(pl.program_id(1) == 0)
    def _():
        os_ref[...] = jnp.dot(a_ref[...], ws_ref[...], preferred_element_type=F32)


def _matmul(a, w, out_dtype, side_w=None, tm=1024, tn=1024, name="proj"):
    m, k = a.shape
    n = w.shape[1]
    tm, tn = min(tm, m), min(tn, n)
    grid = (m // tm, n // tn)
    a_spec = pl.BlockSpec((tm, k), lambda i, j: (i, 0))
    w_spec = pl.BlockSpec((k, tn), lambda i, j: (0, j))
    o_spec = pl.BlockSpec((tm, tn), lambda i, j: (i, j))
    est = 2 * (tm * k * 2 + k * tn * 2 + tm * tn * 2) + tm * tn * 4
    params = pltpu.CompilerParams(dimension_semantics=("arbitrary", "arbitrary"),
                                  vmem_limit_bytes=_vmem_limit(est + 4 * 1024 * 1024))
    if side_w is None:
        return pl.pallas_call(
            _mm_kernel, grid=grid, in_specs=[a_spec, w_spec], out_specs=o_spec,
            out_shape=jax.ShapeDtypeStruct((m, n), out_dtype), compiler_params=params, name=name,
        )(a, w)
    ns = side_w.shape[1]
    return pl.pallas_call(
        _mm_side_kernel, grid=grid,
        in_specs=[a_spec, w_spec, pl.BlockSpec((k, ns), lambda i, j: (0, 0))],
        out_specs=[o_spec, pl.BlockSpec((tm, ns), lambda i, j: (i, 0))],
        out_shape=[jax.ShapeDtypeStruct((m, n), out_dtype), jax.ShapeDtypeStruct((m, ns), F32)],
        compiler_params=params, name=name,
    )(a, w, side_w)


def _pool_kernel(u_ref, z_ref, maps_ref, scale_ref, o_ref, buf_a, buf_b, halo, *, tm, tiles_per_seq, cg):
    t = pl.program_id(0) % tiles_per_seq
    hl = POOL_HALO
    ext = hl + tm
    pos = (t * tm + 1 + lax.broadcasted_iota(jnp.int32, (tm, 1), 0)).astype(F32)

    @pl.when(t == 0)
    def _():
        halo[...] = jnp.zeros_like(halo)

    for g, w in enumerate(POOL_WINDOWS):
        cols = slice(g * cg, (g + 1) * cg)
        u = u_ref[:, cols].astype(F32)
        buf_a[0:hl, :] = halo[:, cols]
        buf_a[hl:ext, :] = u
        halo[:, cols] = u[tm - hl:, :]
        src, dst, lo, shift = buf_a, buf_b, 0, 1
        while shift < w:
            lo += 8
            dst[lo:ext, :] = src[lo:ext, :] + src[lo - shift:ext - shift, :]
            src, dst, shift = dst, src, shift * 2
        inv_cnt = 1.0 / jnp.minimum(pos, float(w))
        d = (src[hl:ext, :] * inv_cnt - u).astype(BF16)
        y = jnp.dot(d, maps_ref[g], preferred_element_type=F32)
        o_ref[:, cols] = (y * scale_ref[:, cols] * _silu(z_ref[:, cols].astype(F32))).astype(o_ref.dtype)


def _pool_mix(proj, maps, scale, seq, bw, tm=256):
    n = proj.shape[0]
    tm = min(tm, seq)
    cg = bw // len(POOL_WINDOWS)
    kern = functools.partial(_pool_kernel, tm=tm, tiles_per_seq=seq // tm, cg=cg)
    est = 2 * (3 * tm * bw * 2 + maps.size * 2) + 2 * (POOL_HALO + tm) * cg * 4 + POOL_HALO * bw * 4 + 4 * tm * cg * 4
    return pl.pallas_call(
        kern,
        grid=(n // tm,),
        in_specs=[
            pl.BlockSpec((tm, bw), lambda i: (i, 0)),
            pl.BlockSpec((tm, bw), lambda i: (i, 1)),
            pl.BlockSpec(maps.shape, lambda i: (0, 0, 0)),
            pl.BlockSpec((1, bw), lambda i: (0, 0)),
        ],
        out_specs=pl.BlockSpec((tm, bw), lambda i: (i, 0)),
        out_shape=jax.ShapeDtypeStruct((n, bw), BF16),
        scratch_shapes=[
            pltpu.VMEM((POOL_HALO + tm, cg), F32),
            pltpu.VMEM((POOL_HALO + tm, cg), F32),
            pltpu.VMEM((POOL_HALO, bw), F32),
        ],
        compiler_params=pltpu.CompilerParams(dimension_semantics=("arbitrary",), vmem_limit_bytes=_vmem_limit(est)),
        name="pool_mix",
    )(proj, proj, maps, scale.reshape(1, bw).astype(F32))


def _xattn_kernel(q_ref, z_ref, k_ref, v_ref, o_ref, *, hd):
    scale = hd ** -0.5
    for h in range(XA_HEADS):
        cols = slice(h * hd, (h + 1) * hd)
        s = lax.dot_general(q_ref[:, cols], k_ref[:, cols], (((1,), (1,)), ((), ())),
                            preferred_element_type=F32) * scale
        e = jnp.exp(s - jnp.max(s, axis=-1, keepdims=True))
        p = e / jnp.sum(e, axis=-1, keepdims=True)
        o = jnp.dot(p.astype(BF16), v_ref[:, cols], preferred_element_type=F32)
        o_ref[:, cols] = (o * _silu(z_ref[:, cols].astype(F32))).astype(o_ref.dtype)


def _xattn(proj, kv, layer, seq, mem_tokens, q_blk, z_blk, xw, tm=512):
    n = proj.shape[0]
    tm = min(tm, seq)
    tps = seq // tm
    kern = functools.partial(_xattn_kernel, hd=xw // XA_HEADS)
    est = 2 * (3 * tm * xw * 2 + 2 * mem_tokens * xw * 2) + 6 * tm * mem_tokens * 4
    return pl.pallas_call(
        kern,
        grid=(n // tm,),
        in_specs=[
            pl.BlockSpec((tm, xw), lambda i: (i, q_blk)),
            pl.BlockSpec((tm, xw), lambda i: (i, z_blk)),
            pl.BlockSpec((mem_tokens, xw), lambda i: (i // tps, 2 * layer)),
            pl.BlockSpec((mem_tokens, xw), lambda i: (i // tps, 2 * layer + 1)),
        ],
        out_specs=pl.BlockSpec((tm, xw), lambda i: (i, 0)),
        out_shape=jax.ShapeDtypeStruct((n, xw), BF16),
        compiler_params=pltpu.CompilerParams(dimension_semantics=("arbitrary",), vmem_limit_bytes=_vmem_limit(est)),
        name="mem_xattn",
    )(proj, proj, kv, kv)


def _out_kernel(br_ref, mo_ref, w_ref, h_ref, g_ref, *out_refs, bw, final):
    acc = jnp.dot(br_ref[...], w_ref[0:bw, :], preferred_element_type=F32)
    acc = acc + jnp.dot(mo_ref[...], w_ref[bw:, :], preferred_element_type=F32)
    hn = h_ref[...] + acc
    y = hn * lax.rsqrt(jnp.mean(hn * hn, axis=-1, keepdims=True) + EPS) * g_ref[...]
    if final:
        out_refs[0][...] = y
    else:
        out_refs[0][...] = hn
        out_refs[1][...] = y.astype(out_refs[1].dtype)


def _out_proj(branch, memo, w_out, h, g_next, final, tm=256):
    n, bw = branch.shape
    xw = memo.shape[1]
    d = w_out.shape[1]
    tm = min(tm, n)
    kern = functools.partial(_out_kernel, bw=bw, final=final)
    row = lambda i: (i, 0)
    if final:
        out_specs = [pl.BlockSpec((tm, d), row)]
        out_shape = [jax.ShapeDtypeStruct((n, d), F32)]
    else:
        out_specs = [pl.BlockSpec((tm, d), row), pl.BlockSpec((tm, d), row)]
        out_shape = [jax.ShapeDtypeStruct((n, d), F32), jax.ShapeDtypeStruct((n, d), BF16)]
    est = w_out.size * 2 + 2 * (tm * (bw + xw) * 2 + tm * d * 4 * 2 + tm * d * 2) + 3 * tm * d * 4
    return pl.pallas_call(
        kern,
        grid=(n // tm,),
        in_specs=[
            pl.BlockSpec((tm, bw), row),
            pl.BlockSpec((tm, xw), row),
            pl.BlockSpec(w_out.shape, lambda i: (0, 0), pipeline_mode=pl.Buffered(1)),
            pl.BlockSpec((tm, d), row),
            pl.BlockSpec((1, d), lambda i: (0, 0)),
        ],
        out_specs=out_specs,
        out_shape=out_shape,
        compiler_params=pltpu.CompilerParams(dimension_semantics=("arbitrary",), vmem_limit_bytes=_vmem_limit(est)),
        name="out_proj",
    )(branch, memo, w_out, h, g_next.reshape(1, d).astype(F32))


def _bdot(a, b):
    return jnp.dot(a.astype(BF16), b.astype(BF16), preferred_element_type=F32)


def _delta_kernel(qkv_ref, z_ref, ba_ref, cw_ref, alog_ref, dtb_ref, ng_ref, o_ref,
                  halo, ext, state, msk_s, *, c, n_qk, strip):
    dk = DN_HEAD_DIM
    rep = 2
    kw = n_qk * dk
    row = lax.broadcasted_iota(jnp.int32, (c, c), 0)
    col = lax.broadcasted_iota(jnp.int32, (c, c), 1)
    causal = row >= col
    strict = row > col
    eye = (row == col).astype(F32)

    @pl.when(pl.program_id(1) == 0)
    def _():
        halo[...] = jnp.zeros_like(halo)
        state[...] = jnp.zeros_like(state)
        prev = None
        for lvl, sh in enumerate((4, 5, 6, 7)):
            same = ((row >> sh) == (col >> sh)).astype(F32)
            msk_s[lvl] = same if prev is None else same - prev
            prev = same

    beta = jax.nn.sigmoid(ba_ref[:, 0:LANES])
    a_in = ba_ref[:, LANES:2 * LANES] + dtb_ref[...]
    softplus = jnp.maximum(a_in, 0.0) + jnp.log1p(jnp.exp(-jnp.abs(a_in)))
    g = -jnp.exp(alog_ref[...]) * softplus
    gc = jnp.dot(causal.astype(F32), g, precision=lax.Precision.HIGHEST, preferred_element_type=F32)
    eg = jnp.exp(gc)
    egl = jnp.exp(gc[c - 1:c, :] - gc)
    gct = gc.T
    eg_last = jnp.exp(jnp.broadcast_to(gct[:, c - 1:c], (LANES, LANES)))
    bscale = beta * eg

    for s0 in range(0, qkv_ref.shape[1], strip):
        cs = slice(s0, s0 + strip)
        x = qkv_ref[:, cs].astype(F32)
        ext[0:CONV_HALO, cs] = halo[:, cs]
        ext[CONV_HALO:CONV_HALO + c, cs] = x
        halo[:, cs] = x[c - CONV_HALO:, :]

    def conv_act(c0):
        cs = slice(c0, c0 + dk)
        y = ext[CONV_HALO:CONV_HALO + c, cs] * cw_ref[DN_CONV_WIDTH - 1:DN_CONV_WIDTH, cs]
        for j in range(1, DN_CONV_WIDTH):
            y = y + ext[CONV_HALO - j:CONV_HALO - j + c, cs] * cw_ref[DN_CONV_WIDTH - 1 - j:DN_CONV_WIDTH - j, cs]
        return _silu(y)

    def l2n(a):
        return a * lax.rsqrt(jnp.sum(a * a, axis=-1, keepdims=True) + EPS)

    ng = ng_ref[...]
    nt = (((1,), (1,)), ((), ()))
    for j0 in range(0, n_qk, DELTA_GROUP_QK):
        js = list(range(j0, min(j0 + DELTA_GROUP_QK, n_qk)))
        hvs = [rep * j + r for j in js for r in range(rep)]
        q = {j: l2n(conv_act(j * dk)) * (dk ** -0.5) for j in js}
        k = {j: l2n(conv_act(kw + j * dk)) for j in js}
        kb = {j: k[j].astype(BF16) for j in js}
        p = {j: lax.dot_general(jnp.concatenate([q[j].astype(BF16), kb[j]], axis=0), kb[j], nt,
                                preferred_element_type=F32) for j in js}
        dm, a_mat = {}, {}
        for hv in hvs:
            diff = gc[:, hv:hv + 1] - gct[hv:hv + 1, :]
            dm[hv] = jnp.where(causal, jnp.exp(jnp.where(causal, diff, 0.0)), 0.0)
            a_mat[hv] = jnp.where(strict, p[hv // rep][c:2 * c] * dm[hv] * beta[:, hv:hv + 1], 0.0)
        n1 = {hv: -(a_mat[hv] * msk_s[0]) for hv in hvs}
        t = {hv: eye + n1[hv] for hv in hvs}
        pw = n1
        for _ in range(3):
            pw = {hv: _bdot(pw[hv], pw[hv]) for hv in hvs}
            t = {hv: t[hv] + _bdot(t[hv], pw[hv]) for hv in hvs}
        for lvl in range(1, msk_s.shape[0]):
            m = {hv: _bdot(a_mat[hv] * msk_s[lvl], t[hv]) for hv in hvs}
            t = {hv: t[hv] - _bdot(t[hv], m[hv]) for hv in hvs}
        uw = {}
        for hv in hvs:
            v = conv_act(2 * kw + hv * dk)
            rhs = jnp.concatenate([(v * beta[:, hv:hv + 1]).astype(BF16),
                                   (k[hv // rep] * bscale[:, hv:hv + 1]).astype(BF16)], axis=1)
            uw[hv] = jnp.dot(t[hv].astype(BF16), rhs, preferred_element_type=F32)
        st = {hv: state[hv] for hv in hvs}
        wq = {hv: jnp.dot(jnp.concatenate([uw[hv][:, dk:2 * dk].astype(BF16),
                                           (q[hv // rep] * eg[:, hv:hv + 1]).astype(BF16)], axis=0),
                          st[hv].astype(BF16), preferred_element_type=F32) for hv in hvs}
        v_new = {hv: (uw[hv][:, 0:dk] - wq[hv][0:c]).astype(BF16) for hv in hvs}
        o = {hv: wq[hv][c:2 * c] + jnp.dot((p[hv // rep][0:c] * dm[hv]).astype(BF16), v_new[hv],
                                           preferred_element_type=F32) for hv in hvs}
        for hv in hvs:
            kdt = (k[hv // rep] * egl[:, hv:hv + 1]).T.astype(BF16)
            state[hv] = st[hv] * eg_last[hv:hv + 1, :] + jnp.dot(kdt, v_new[hv], preferred_element_type=F32)
        for hv in hvs:
            cols = slice(hv * dk, (hv + 1) * dk)
            on = o[hv] * lax.rsqrt(jnp.mean(o[hv] * o[hv], axis=-1, keepdims=True) + EPS) * ng
            o_ref[:, cols] = (on * _silu(z_ref[:, cols].astype(F32))).astype(o_ref.dtype)


def _delta_mix(proj, ba, conv_w, a_log, dt_bias, norm_g, batch, seq, bw):
    n = proj.shape[0]
    c = min(DELTA_CHUNK, seq)
    dk = DN_HEAD_DIM
    n_v = bw // dk
    n_qk = n_v // 2
    cw = 2 * n_qk * dk + bw
    nch = seq // c
    strip = 512
    kern = functools.partial(_delta_kernel, c=c, n_qk=n_qk, strip=strip)

    def pad_lanes(v):
        return jnp.zeros((1, LANES), F32).at[0, :v.shape[0]].set(v.astype(F32))

    rowblk = lambda b, t: (b * nch + t, 0)
    est = (2 * (c * cw * 2 + 2 * c * bw * 2 + c * 2 * LANES * 4) + (2 * CONV_HALO + c) * cw * 4
           + n_v * dk * dk * 4 + 4 * c * c * 4 + 16 * 1024 * 1024)
    return pl.pallas_call(
        kern,
        grid=(batch, nch),
        in_specs=[
            pl.BlockSpec((c, cw), rowblk),
            pl.BlockSpec((c, bw), lambda b, t: (b * nch + t, cw // bw)),
            pl.BlockSpec((c, 2 * LANES), rowblk),
            pl.BlockSpec((DN_CONV_WIDTH, cw), lambda b, t: (0, 0)),
            pl.BlockSpec((1, LANES), lambda b, t: (0, 0)),
            pl.BlockSpec((1, LANES), lambda b, t: (0, 0)),
            pl.BlockSpec((1, dk), lambda b, t: (0, 0)),
        ],
        out_specs=pl.BlockSpec((c, bw), rowblk),
        out_shape=jax.ShapeDtypeStruct((n, bw), BF16),
        scratch_shapes=[
            pltpu.VMEM((CONV_HALO, cw), F32),
            pltpu.VMEM((CONV_HALO + c, cw), F32),
            pltpu.VMEM((n_v, dk, dk), F32),
            pltpu.VMEM((4, c, c), F32),
        ],
        compiler_params=pltpu.CompilerParams(dimension_semantics=("arbitrary", "arbitrary"),
                                             vmem_limit_bytes=_vmem_limit(est)),
        name="delta_mix",
    )(proj, proj, ba, conv_w.astype(F32), pad_lanes(a_log), pad_lanes(dt_bias),
      norm_g.reshape(1, dk).astype(F32))


def kernel(x, mem, layer_norm_g, mem_norm_g, final_norm_g, w_in_pool, pool_maps, pool_scale, w_in_delta,
           dn_conv_w, dn_a_log, dn_dt_bias, dn_norm_g, w_mem_kv, w_out):
    batch, seq, d = x.shape
    mem_tokens = mem.shape[1]
    depth = layer_norm_g.shape[0]
    n = batch * seq
    bw = pool_scale.shape[1]
    xw = w_mem_kv.shape[2] // 2
    n_v = dn_a_log.shape[1]
    main_w = w_in_delta.shape[2] - 2 * n_v

    mem_n = _rmsnorm(mem.reshape(batch * mem_tokens, d), mem_norm_g, BF16)
    w_kv_all = jnp.transpose(w_mem_kv, (1, 0, 2)).reshape(d, depth * 2 * xw).astype(BF16)
    kv = _matmul(mem_n, w_kv_all, BF16, name="mem_kv")

    h = x.reshape(n, d)
    xn = _rmsnorm(h, layer_norm_g[0], BF16)
    for layer in range(depth):
        j = layer // 2
        if layer % 2 == 0:
            proj = _matmul(xn, w_in_pool[j].astype(BF16), BF16, name="in_proj_pool")
            branch = _pool_mix(proj, pool_maps[j].astype(BF16), pool_scale[j], seq, bw)
            z_mem_blk, q_blk = (bw + bw) // xw, (bw + bw + xw) // xw
        else:
            w = w_in_delta[j]
            side = jnp.zeros((d, 2 * LANES), BF16)
            side = side.at[:, 0:n_v].set(w[:, main_w:main_w + n_v].astype(BF16))
            side = side.at[:, LANES:LANES + n_v].set(w[:, main_w + n_v:].astype(BF16))
            proj, ba = _matmul(xn, w[:, :main_w].astype(BF16), BF16, side_w=side, name="in_proj_delta")
            branch = _delta_mix(proj, ba, dn_conv_w[j], dn_a_log[j], dn_dt_bias[j], dn_norm_g[j], batch, seq, bw)
            cwid = main_w - (bw + xw) - xw
            z_mem_blk, q_blk = (cwid + bw) // xw, (cwid + bw + xw) // xw
        memo = _xattn(proj, kv, layer, seq, mem_tokens, q_blk, z_mem_blk, xw)
        final = layer == depth - 1
        g_next = final_norm_g if final else layer_norm_g[layer + 1]
        outs = _out_proj(branch, memo, w_out[layer].astype(BF16), h, g_next, final)
        if final:
            return outs[0].reshape(batch, seq, d)
        h, xn = outs
```

```python
import functools

import jax
import jax.numpy as jnp
from jax import lax
from jax.experimental import pallas as pl
from jax.experimental.pallas import tpu as pltpu

F32 = jnp.float32
BF16 = jnp.bfloat16
EPS = 1e-6
LOG2E = 1.4426950408889634

POOL_WINDOWS = (2, 4, 8, 16)
POOL_HALO = 32
XA_HEADS = 4
DN_HEAD_DIM = 128
DN_CONV_WIDTH = 4
DELTA_CHUNK = 128
DELTA_GROUP_QK = 8
DELTA_BASE_BLOCK = 8
SUBLANES = 8
LANES = 128
MASKED = -1e30

V7X_VMEM_BYTES = 64 * 1024 * 1024


def _vmem_limit(estimate_bytes):
    return int(min(max(estimate_bytes * 5 // 4, 16 * 1024 * 1024), V7X_VMEM_BYTES - 6 * 1024 * 1024))


def _silu_of_half(h):
    return h + h * jnp.tanh(h)


def _silu(x):
    return _silu_of_half(0.5 * x)


def _shift_rows(cur, prev, s):
    r, w = cur.shape
    tiles = (r // SUBLANES, SUBLANES, w)
    wraps = lax.broadcasted_iota(jnp.int32, (1, SUBLANES, w), 1) >= SUBLANES - s
    m = jnp.where(wraps, prev.reshape(tiles), cur.reshape(tiles))
    return pltpu.roll(m, s, axis=1).reshape(r, w)


def _rmsnorm_kernel(x_ref, g_ref, o_ref):
    x = x_ref[...].astype(F32)
    ms = jnp.mean(x * x, axis=-1, keepdims=True)
    o_ref[...] = (x * lax.rsqrt(ms + EPS) * g_ref[...]).astype(o_ref.dtype)


def _rmsnorm(x2d, g, out_dtype, tm=512):
    n, d = x2d.shape
    tm = min(tm, n)
    return pl.pallas_call(
        _rmsnorm_kernel,
        grid=(n // tm,),
        in_specs=[pl.BlockSpec((tm, d), lambda i: (i, 0)), pl.BlockSpec((1, d), lambda i: (0, 0))],
        out_specs=pl.BlockSpec((tm, d), lambda i: (i, 0)),
        out_shape=jax.ShapeDtypeStruct((n, d), out_dtype),
        compiler_params=pltpu.CompilerParams(dimension_semantics=("arbitrary",)),
        name="rmsnorm_cast",
    )(x2d, g.reshape(1, d).astype(F32))


def _mm_kernel(a_ref, w_ref, o_ref):
    o_ref[...] = jnp.dot(a_ref[...], w_ref[...], preferred_element_type=F32).astype(o_ref.dtype)


def _mm_side_kernel(a_ref, w_ref, ws_ref, o_ref, os_ref):
    o_ref[...] = jnp.dot(a_ref[...], w_ref[...], preferred_element_type=F32).astype(o_ref.dtype)

    @pl.when(pl.program_id(1) == 0)
    def _():
        os_ref[...] = jnp.dot(a_ref[...], ws_ref[...], preferred_element_type=F32)


def _matmul(a, w3, n_layers, n_out, out_dtype, layer0=0, side_w=None, tm=1024, tn=1024, name="proj"):
    m, k = a.shape
    tm, tn = min(tm, m), min(tn, n_out)
    npl = n_out // tn
    grid = (m // tm, n_layers * npl)
    a_spec = pl.BlockSpec((tm, k), lambda i, j: (i, 0))
    w_spec = pl.BlockSpec((None, k, tn), lambda i, j: (layer0 + j // npl, 0, j % npl))
    o_spec = pl.BlockSpec((tm, tn), lambda i, j: (i, j))
    n_total = n_layers * n_out
    est = 2 * (tm * k * 2 + k * tn * 2 + tm * tn * 2) + tm * tn * 4
    params = pltpu.CompilerParams(dimension_semantics=("arbitrary", "arbitrary"),
                                  vmem_limit_bytes=_vmem_limit(est + 4 * 1024 * 1024))
    if side_w is None:
        return pl.pallas_call(
            _mm_kernel, grid=grid, in_specs=[a_spec, w_spec], out_specs=o_spec,
            out_shape=jax.ShapeDtypeStruct((m, n_total), out_dtype), compiler_params=params, name=name,
        )(a, w3)
    ns = side_w.shape[1]
    return pl.pallas_call(
        _mm_side_kernel, grid=grid,
        in_specs=[a_spec, w_spec, pl.BlockSpec((k, ns), lambda i, j: (0, 0))],
        out_specs=[o_spec, pl.BlockSpec((tm, ns), lambda i, j: (i, 0))],
        out_shape=[jax.ShapeDtypeStruct((m, n_total), out_dtype), jax.ShapeDtypeStruct((m, ns), F32)],
        compiler_params=params, name=name,
    )(a, w3, side_w)


def _pool_kernel(u_ref, z_ref, maps_ref, scale_ref, o_ref, buf_a, buf_b, halo, *, tm, tiles_per_seq, cg):
    t = pl.program_id(0) % tiles_per_seq
    hl = POOL_HALO
    ext = hl + tm
    pos = (t * tm + 1 + lax.broadcasted_iota(jnp.int32, (tm, 1), 0)).astype(F32)

    @pl.when(t == 0)
    def _():
        halo[...] = jnp.zeros_like(halo)

    for g, w in enumerate(POOL_WINDOWS):
        cols = slice(g * cg, (g + 1) * cg)
        u = u_ref[:, cols].astype(F32)
        buf_a[0:hl, :] = halo[:, cols]
        buf_a[hl:ext, :] = u
        halo[:, cols] = u[tm - hl:, :]
        src, dst, lo, shift = buf_a, buf_b, 0, 1
        while shift < w:
            lo += SUBLANES
            cur = src[lo:ext, :]
            prev = src[lo - SUBLANES:ext - SUBLANES, :]
            dst[lo:ext, :] = cur + (_shift_rows(cur, prev, shift) if shift < SUBLANES else prev)
            src, dst, shift = dst, src, shift * 2
        inv_cnt = 1.0 / jnp.minimum(pos, float(w))
        d = (src[hl:ext, :] * inv_cnt - u).astype(BF16)
        y = jnp.dot(d, maps_ref[g], preferred_element_type=F32)
        o_ref[:, cols] = (y * scale_ref[:, cols] * _silu(z_ref[:, cols].astype(F32))).astype(o_ref.dtype)


def _pool_mix(proj, maps4, layer, scale, seq, bw, tm=256):
    n = proj.shape[0]
    tm = min(tm, seq)
    ng, cg = maps4.shape[1], maps4.shape[2]
    kern = functools.partial(_pool_kernel, tm=tm, tiles_per_seq=seq // tm, cg=cg)
    est = 2 * (3 * tm * bw * 2 + ng * cg * cg * 2) + 2 * (POOL_HALO + tm) * cg * 4 + POOL_HALO * bw * 4 + 4 * tm * cg * 4
    return pl.pallas_call(
        kern,
        grid=(n // tm,),
        in_specs=[
            pl.BlockSpec((tm, bw), lambda i: (i, 0)),
            pl.BlockSpec((tm, bw), lambda i: (i, 1)),
            pl.BlockSpec((None, ng, cg, cg), lambda i: (layer, 0, 0, 0)),
            pl.BlockSpec((1, bw), lambda i: (0, 0)),
        ],
        out_specs=pl.BlockSpec((tm, bw), lambda i: (i, 0)),
        out_shape=jax.ShapeDtypeStruct((n, bw), BF16),
        scratch_shapes=[
            pltpu.VMEM((POOL_HALO + tm, cg), F32),
            pltpu.VMEM((POOL_HALO + tm, cg), F32),
            pltpu.VMEM((POOL_HALO, bw), F32),
        ],
        compiler_params=pltpu.CompilerParams(dimension_semantics=("arbitrary",), vmem_limit_bytes=_vmem_limit(est)),
        name="pool_mix",
    )(proj, proj, maps4, scale.reshape(1, bw).astype(F32))


def _xattn_kernel(q_ref, z_ref, k_ref, v_ref, o_ref, *, hd):
    scale = hd ** -0.5
    for h in range(XA_HEADS):
        cols = slice(h * hd, (h + 1) * hd)
        s = lax.dot_general(q_ref[:, cols], k_ref[:, cols], (((1,), (1,)), ((), ())),
                            preferred_element_type=F32) * scale
        e = jnp.exp(s - jnp.max(s, axis=-1, keepdims=True))
        p = e * (1.0 / jnp.sum(e, axis=-1, keepdims=True))
        o = jnp.dot(p.astype(BF16), v_ref[:, cols], preferred_element_type=F32)
        o_ref[:, cols] = (o * _silu(z_ref[:, cols].astype(F32))).astype(o_ref.dtype)


def _xattn(proj, kv, layer, seq, mem_tokens, q_blk, z_blk, xw, tm=512):
    n = proj.shape[0]
    tm = min(tm, seq)
    tps = seq // tm
    kern = functools.partial(_xattn_kernel, hd=xw // XA_HEADS)
    est = 2 * (3 * tm * xw * 2 + 2 * mem_tokens * xw * 2) + 6 * tm * mem_tokens * 4
    return pl.pallas_call(
        kern,
        grid=(n // tm,),
        in_specs=[
            pl.BlockSpec((tm, xw), lambda i: (i, q_blk)),
            pl.BlockSpec((tm, xw), lambda i: (i, z_blk)),
            pl.BlockSpec((mem_tokens, xw), lambda i: (i // tps, 2 * layer)),
            pl.BlockSpec((mem_tokens, xw), lambda i: (i // tps, 2 * layer + 1)),
        ],
        out_specs=pl.BlockSpec((tm, xw), lambda i: (i, 0)),
        out_shape=jax.ShapeDtypeStruct((n, xw), BF16),
        compiler_params=pltpu.CompilerParams(dimension_semantics=("arbitrary",), vmem_limit_bytes=_vmem_limit(est)),
        name="mem_xattn",
    )(proj, proj, kv, kv)


def _out_kernel(br_ref, mo_ref, w_ref, h_ref, g_ref, *out_refs, bw, final):
    acc = jnp.dot(br_ref[...], w_ref[0:bw, :], preferred_element_type=F32)
    acc = acc + jnp.dot(mo_ref[...], w_ref[bw:, :], preferred_element_type=F32)
    hn = h_ref[...] + acc
    y = hn * lax.rsqrt(jnp.mean(hn * hn, axis=-1, keepdims=True) + EPS) * g_ref[...]
    if final:
        out_refs[0][...] = y
    else:
        out_refs[0][...] = hn
        out_refs[1][...] = y.astype(out_refs[1].dtype)


def _out_proj(branch, memo, w_out3, layer, h, g_next, final, tm=256):
    n, bw = branch.shape
    xw = memo.shape[1]
    mw, d = w_out3.shape[1], w_out3.shape[2]
    tm = min(tm, n)
    kern = functools.partial(_out_kernel, bw=bw, final=final)
    row = lambda i: (i, 0)
    if final:
        out_specs = [pl.BlockSpec((tm, d), row)]
        out_shape = [jax.ShapeDtypeStruct((n, d), F32)]
    else:
        out_specs = [pl.BlockSpec((tm, d), row), pl.BlockSpec((tm, d), row)]
        out_shape = [jax.ShapeDtypeStruct((n, d), F32), jax.ShapeDtypeStruct((n, d), BF16)]
    est = mw * d * 2 + 2 * (tm * (bw + xw) * 2 + tm * d * 4 * 2 + tm * d * 2) + 3 * tm * d * 4
    return pl.pallas_call(
        kern,
        grid=(n // tm,),
        in_specs=[
            pl.BlockSpec((tm, bw), row),
            pl.BlockSpec((tm, xw), row),
            pl.BlockSpec((None, mw, d), lambda i: (layer, 0, 0), pipeline_mode=pl.Buffered(1)),
            pl.BlockSpec((tm, d), row),
            pl.BlockSpec((1, d), lambda i: (0, 0)),
        ],
        out_specs=out_specs,
        out_shape=out_shape,
        compiler_params=pltpu.CompilerParams(dimension_semantics=("arbitrary",), vmem_limit_bytes=_vmem_limit(est)),
        name="out_proj",
    )(branch, memo, w_out3, h, g_next.reshape(1, d).astype(F32))


def _odd_blocks(x, b):
    return jnp.concatenate([x[i:i + b] for i in range(b, x.shape[0], 2 * b)], axis=0)


def _odd_scatter(y, b):
    zero = jnp.zeros((b, y.shape[1]), y.dtype)
    parts = []
    for i in range(0, y.shape[0], b):
        parts += [zero, y[i:i + b]]
    return jnp.concatenate(parts, axis=0)


def _odd_merge(x, y, b):
    parts = []
    for m, i in enumerate(range(0, x.shape[0], 2 * b)):
        parts += [x[i:i + b], y[m * b:(m + 1) * b]]
    return jnp.concatenate(parts, axis=0)


def _delta_kernel(qkv_ref, z_ref, ba_ref, cw_ref, alog_ref, dtb_ref, ng_ref, o_ref,
                  halo, ext, state, msk_s, *, c, n_qk, strip):
    dk = DN_HEAD_DIM
    rep = 2
    kw = n_qk * dk
    hl = SUBLANES
    half = c // 2
    levels = []
    b = DELTA_BASE_BLOCK
    while b < c:
        levels.append(b)
        b *= 2
    row = lax.broadcasted_iota(jnp.int32, (c, c), 0)
    col = lax.broadcasted_iota(jnp.int32, (c, c), 1)
    eye = (row == col).astype(F32)
    not_causal = jnp.where(row >= col, 0.0, MASKED)

    @pl.when(pl.program_id(1) == 0)
    def _():
        halo[...] = jnp.zeros_like(halo)
        state[...] = jnp.zeros_like(state)
        sh0 = DELTA_BASE_BLOCK.bit_length() - 1
        msk_s[0:c, :] = jnp.where(((row >> sh0) == (col >> sh0)) & (row > col), -1.0, 0.0)
        rr = lax.broadcasted_iota(jnp.int32, (half, c), 0)
        cc = lax.broadcasted_iota(jnp.int32, (half, c), 1)
        for lvl, blk in enumerate(levels):
            sh = blk.bit_length() - 1
            msk_s[c + lvl * half:c + (lvl + 1) * half, :] = ((cc >> sh) == 2 * (rr >> sh)).astype(F32)

    beta = jax.nn.sigmoid(ba_ref[:, 0:LANES])
    a_in = ba_ref[:, LANES:2 * LANES] + dtb_ref[...]
    softplus = jnp.maximum(a_in, 0.0) + jnp.log1p(jnp.exp(-jnp.abs(a_in)))
    g = -jnp.exp(alog_ref[...]) * softplus
    gc = jnp.dot((row >= col).astype(F32), g, precision=lax.Precision.HIGHEST, preferred_element_type=F32)
    eg = jnp.exp(gc)
    egl = jnp.exp(gc[c - 1:c, :] - gc)
    gct = gc.T
    eg_last = jnp.exp(jnp.broadcast_to(gct[:, c - 1:c], (LANES, LANES)))
    bscale = beta * eg
    g2 = gc * LOG2E
    g2t = gct * LOG2E

    for s0 in range(0, qkv_ref.shape[1], strip):
        cs = slice(s0, s0 + strip)
        x = qkv_ref[:, cs].astype(F32)
        ext[0:hl, cs] = halo[:, cs]
        ext[hl:hl + c, cs] = x
        halo[:, cs] = x[c - hl:, :]

    def conv_act(c0):
        cs = slice(c0, c0 + dk)
        cur = ext[hl:hl + c, cs]
        prev = ext[0:c, cs]
        half_w = 0.5 * cw_ref[:, cs]
        y = cur * half_w[DN_CONV_WIDTH - 1:DN_CONV_WIDTH]
        for j in range(1, DN_CONV_WIDTH):
            y = y + _shift_rows(cur, prev, j) * half_w[DN_CONV_WIDTH - 1 - j:DN_CONV_WIDTH - j]
        return _silu_of_half(y)

    def l2n(a, scale=1.0):
        return a * (lax.rsqrt(jnp.sum(a * a, axis=-1, keepdims=True) + EPS) * scale)

    def dot(a, b):
        return jnp.dot(a, b, preferred_element_type=F32)

    ng = ng_ref[...]
    nt = (((1,), (1,)), ((), ()))
    for j0 in range(0, n_qk, DELTA_GROUP_QK):
        js = list(range(j0, min(j0 + DELTA_GROUP_QK, n_qk)))
        hvs = [rep * j + r for j in js for r in range(rep)]
        q = {j: l2n(conv_act(j * dk), dk ** -0.5) for j in js}
        k = {j: l2n(conv_act(kw + j * dk)) for j in js}
        kb = {j: k[j].astype(BF16) for j in js}
        p = {j: lax.dot_general(jnp.concatenate([q[j].astype(BF16), kb[j]], axis=0), kb[j], nt,
                                preferred_element_type=F32) for j in js}
        dm, a_raw = {}, {}
        for hv in hvs:
            dm[hv] = jnp.exp2(g2[:, hv:hv + 1] - g2t[hv:hv + 1, :] + not_causal)
            a_raw[hv] = p[hv // rep][c:2 * c] * (dm[hv] * beta[:, hv:hv + 1])
        n1 = {hv: a_raw[hv] * msk_s[0:c, :] for hv in hvs}
        t = {hv: eye + n1[hv] for hv in hvs}
        pw = {hv: n1[hv].astype(BF16) for hv in hvs}
        sq = 1
        while 2 * sq < DELTA_BASE_BLOCK:
            pw = {hv: dot(pw[hv], pw[hv]).astype(BF16) for hv in hvs}
            t = {hv: t[hv] + dot(t[hv].astype(BF16), pw[hv]) for hv in hvs}
            sq *= 2
        for lvl, blk in enumerate(levels):
            lm = msk_s[c + lvl * half:c + (lvl + 1) * half, :]
            m = {hv: dot((_odd_blocks(a_raw[hv], blk) * lm).astype(BF16), t[hv].astype(BF16)) for hv in hvs}
            t = {hv: _odd_merge(t[hv], _odd_blocks(t[hv], blk)
                                - dot(_odd_blocks(t[hv], blk).astype(BF16), _odd_scatter(m[hv], blk).astype(BF16)),
                                blk) for hv in hvs}
        uw = {}
        for hv in hvs:
            v = conv_act(2 * kw + hv * dk)
            rhs = jnp.concatenate([(v * beta[:, hv:hv + 1]).astype(BF16),
                                   (k[hv // rep] * bscale[:, hv:hv + 1]).astype(BF16)], axis=1)
            uw[hv] = dot(t[hv].astype(BF16), rhs)
        st = {hv: state[hv] for hv in hvs}
        wq = {hv: dot(jnp.concatenate([uw[hv][:, dk:2 * dk].astype(BF16),
                                       (q[hv // rep] * eg[:, hv:hv + 1]).astype(BF16)], axis=0),
                      st[hv].astype(BF16)) for hv in hvs}
        v_new = {hv: (uw[hv][:, 0:dk] - wq[hv][0:c]).astype(BF16) for hv in hvs}
        o = {hv: wq[hv][c:2 * c] + dot((p[hv // rep][0:c] * dm[hv]).astype(BF16), v_new[hv]) for hv in hvs}
        for hv in hvs:
            kdt = (k[hv // rep] * egl[:, hv:hv + 1]).T.astype(BF16)
            state[hv] = st[hv] * eg_last[hv:hv + 1, :] + dot(kdt, v_new[hv])
        for hv in hvs:
            cols = slice(hv * dk, (hv + 1) * dk)
            on = o[hv] * lax.rsqrt(jnp.mean(o[hv] * o[hv], axis=-1, keepdims=True) + EPS) * ng
            o_ref[:, cols] = (on * _silu(z_ref[:, cols].astype(F32))).astype(o_ref.dtype)


def _delta_mix(proj, ba, conv_w, a_log, dt_bias, norm_g, batch, seq, bw):
    n = proj.shape[0]
    c = min(DELTA_CHUNK, seq)
    dk = DN_HEAD_DIM
    n_v = bw // dk
    n_qk = n_v // 2
    cw = 2 * n_qk * dk + bw
    nch = seq // c
    strip = 512
    n_lvl = (c // DELTA_BASE_BLOCK).bit_length() - 1
    kern = functools.partial(_delta_kernel, c=c, n_qk=n_qk, strip=strip)

    def pad_lanes(v):
        return jnp.zeros((1, LANES), F32).at[0, :v.shape[0]].set(v.astype(F32))

    rowblk = lambda b, t: (b * nch + t, 0)
    est = (2 * (c * cw * 2 + 2 * c * bw * 2 + c * 2 * LANES * 4) + (2 * SUBLANES + c) * cw * 4
           + n_v * dk * dk * 4 + (2 + n_lvl) * c * c * 2 + 16 * 1024 * 1024)
    return pl.pallas_call(
        kern,
        grid=(batch, nch),
        in_specs=[
            pl.BlockSpec((c, cw), rowblk),
            pl.BlockSpec((c, bw), lambda b, t: (b * nch + t, cw // bw)),
            pl.BlockSpec((c, 2 * LANES), rowblk),
            pl.BlockSpec((DN_CONV_WIDTH, cw), lambda b, t: (0, 0)),
            pl.BlockSpec((1, LANES), lambda b, t: (0, 0)),
            pl.BlockSpec((1, LANES), lambda b, t: (0, 0)),
            pl.BlockSpec((1, dk), lambda b, t: (0, 0)),
        ],
        out_specs=pl.BlockSpec((c, bw), rowblk),
        out_shape=jax.ShapeDtypeStruct((n, bw), BF16),
        scratch_shapes=[
            pltpu.VMEM((SUBLANES, cw), F32),
            pltpu.VMEM((SUBLANES + c, cw), F32),
            pltpu.VMEM((n_v, dk, dk), F32),
            pltpu.VMEM((c + n_lvl * (c // 2), c), F32),
        ],
        compiler_params=pltpu.CompilerParams(dimension_semantics=("arbitrary", "arbitrary"),
                                             vmem_limit_bytes=_vmem_limit(est)),
        name="delta_mix",
    )(proj, proj, ba, conv_w.astype(F32), pad_lanes(a_log), pad_lanes(dt_bias),
      norm_g.reshape(1, dk).astype(F32))


def kernel(x, mem, layer_norm_g, mem_norm_g, final_norm_g, w_in_pool, pool_maps, pool_scale, w_in_delta,
           dn_conv_w, dn_a_log, dn_dt_bias, dn_norm_g, w_mem_kv, w_out):
    batch, seq, d = x.shape
    mem_tokens = mem.shape[1]
    depth = layer_norm_g.shape[0]
    n = batch * seq
    bw = pool_scale.shape[1]
    xw = w_mem_kv.shape[2] // 2
    n_v = dn_a_log.shape[1]
    main_w = w_in_delta.shape[2] - 2 * n_v
    cwid = main_w - (bw + xw) - xw

    w_in_pool_b = w_in_pool.astype(BF16)
    w_in_delta_b = w_in_delta.astype(BF16)
    pool_maps_b = pool_maps.astype(BF16)
    w_mem_kv_b = w_mem_kv.astype(BF16)
    w_out_b = w_out.astype(BF16)

    mem_n = _rmsnorm(mem.reshape(batch * mem_tokens, d), mem_norm_g, BF16)
    kv = _matmul(mem_n, w_mem_kv_b, depth, 2 * xw, BF16, name="mem_kv")

    h = x.reshape(n, d)
    xn = _rmsnorm(h, layer_norm_g[0], BF16)
    for layer in range(depth):
        j = layer // 2
        if layer % 2 == 0:
            proj = _matmul(xn, w_in_pool_b, 1, w_in_pool.shape[2], BF16, layer0=j, name="in_proj_pool")
            branch = _pool_mix(proj, pool_maps_b, j, pool_scale[j], seq, bw)
            z_mem_blk, q_blk = (bw + bw) // xw, (bw + bw + xw) // xw
        else:
            side = jnp.zeros((d, 2 * LANES), BF16)
            side = side.at[:, 0:n_v].set(w_in_delta_b[j, :, main_w:main_w + n_v])
            side = side.at[:, LANES:LANES + n_v].set(w_in_delta_b[j, :, main_w + n_v:])
            proj, ba = _matmul(xn, w_in_delta_b, 1, main_w, BF16, layer0=j, side_w=side, name="in_proj_delta")
            branch = _delta_mix(proj, ba, dn_conv_w[j], dn_a_log[j], dn_dt_bias[j], dn_norm_g[j], batch, seq, bw)
            z_mem_blk, q_blk = (cwid + bw) // xw, (cwid + bw + xw) // xw
        memo = _xattn(proj, kv, layer, seq, mem_tokens, q_blk, z_mem_blk, xw)
        final = layer == depth - 1
        g_next = final_norm_g if final else layer_norm_g[layer + 1]
        outs = _out_proj(branch, memo, w_out_b, layer, h, g_next, final)
        if final:
            return outs[0].reshape(batch, seq, d)
        h, xn = outs
```

```python
import functools

import jax
import jax.numpy as jnp
from jax import lax
from jax.experimental import pallas as pl
from jax.experimental.pallas import tpu as pltpu

F32 = jnp.float32
BF16 = jnp.bfloat16
EPS = 1e-6
LOG2E = 1.4426950408889634

POOL_WINDOWS = (2, 4, 8, 16)
POOL_HALO = 32
XA_HEADS = 4
DN_HEAD_DIM = 128
DN_CONV_WIDTH = 4
DELTA_CHUNK = 128
DELTA_GROUP_QK = 8
DELTA_BASE_BLOCK = 8
SUBLANES = 8
LANES = 128
MASKED = -1e30

V7X_VMEM_BYTES = 64 * 1024 * 1024


def _vmem_limit(estimate_bytes):
    return int(min(max(estimate_bytes * 5 // 4, 16 * 1024 * 1024), V7X_VMEM_BYTES - 6 * 1024 * 1024))


def _silu_of_half(h):
    return h + h * jnp.tanh(h)


def _silu(x):
    return _silu_of_half(0.5 * x)


def _shift_rows(cur, prev, s):
    r, w = cur.shape
    tiles = (r // SUBLANES, SUBLANES, w)
    wraps = lax.broadcasted_iota(jnp.int32, (1, SUBLANES, w), 1) >= SUBLANES - s
    m = jnp.where(wraps, prev.reshape(tiles), cur.reshape(tiles))
    return pltpu.roll(m, s, axis=1).reshape(r, w)


def _rmsnorm_kernel(x_ref, g_ref, o_ref):
    x = x_ref[...].astype(F32)
    ms = jnp.mean(x * x, axis=-1, keepdims=True)
    o_ref[...] = (x * lax.rsqrt(ms + EPS) * g_ref[...]).astype(o_ref.dtype)


def _rmsnorm(x2d, g, out_dtype, tm=512):
    n, d = x2d.shape
    tm = min(tm, n)
    return pl.pallas_call(
        _rmsnorm_kernel,
        grid=(n // tm,),
        in_specs=[pl.BlockSpec((tm, d), lambda i: (i, 0)), pl.BlockSpec((1, d), lambda i: (0, 0))],
        out_specs=pl.BlockSpec((tm, d), lambda i: (i, 0)),
        out_shape=jax.ShapeDtypeStruct((n, d), out_dtype),
        compiler_params=pltpu.CompilerParams(dimension_semantics=("arbitrary",)),
        name="rmsnorm_cast",
    )(x2d, g.reshape(1, d).astype(F32))


def _mm_kernel(a_ref, w_ref, o_ref):
    o_ref[...] = jnp.dot(a_ref[...], w_ref[...], preferred_element_type=F32).astype(o_ref.dtype)


def _mm_side_kernel(a_ref, w_ref, ws_ref, o_ref, os_ref):
    o_ref[...] = jnp.dot(a_ref[...], w_ref[...], preferred_element_type=F32).astype(o_ref.dtype)

    @pl.when(pl.program_id(1) == 0)
    def _():
        os_ref[...] = jnp.dot(a_ref[...], ws_ref[...], preferred_element_type=F32)


def _matmul(a, w3, n_layers, n_out, out_dtype, layer0=0, side_w=None, tm=1024, tn=2048, name="proj"):
    m, k = a.shape
    tm, tn = min(tm, m), min(tn, n_out)
    npl = n_out // tn
    grid = (m // tm, n_layers * npl)
    a_spec = pl.BlockSpec((tm, k), lambda i, j: (i, 0))
    w_spec = pl.BlockSpec((None, k, tn), lambda i, j: (layer0 + j // npl, 0, j % npl))
    o_spec = pl.BlockSpec((tm, tn), lambda i, j: (i, j))
    n_total = n_layers * n_out
    est = 2 * (tm * k * 2 + k * tn * 2 + tm * tn * 2) + tm * tn * 4
    params = pltpu.CompilerParams(dimension_semantics=("arbitrary", "arbitrary"),
                                  vmem_limit_bytes=_vmem_limit(est + 4 * 1024 * 1024))
    if side_w is None:
        return pl.pallas_call(
            _mm_kernel, grid=grid, in_specs=[a_spec, w_spec], out_specs=o_spec,
            out_shape=jax.ShapeDtypeStruct((m, n_total), out_dtype), compiler_params=params, name=name,
        )(a, w3)
    ns = side_w.shape[1]
    return pl.pallas_call(
        _mm_side_kernel, grid=grid,
        in_specs=[a_spec, w_spec, pl.BlockSpec((k, ns), lambda i, j: (0, 0))],
        out_specs=[o_spec, pl.BlockSpec((tm, ns), lambda i, j: (i, 0))],
        out_shape=[jax.ShapeDtypeStruct((m, n_total), out_dtype), jax.ShapeDtypeStruct((m, ns), F32)],
        compiler_params=params, name=name,
    )(a, w3, side_w)


def _pool_kernel(u_ref, z_ref, maps_ref, scale_ref, o_ref, buf_a, buf_b, halo, *, tm, tiles_per_seq, cg):
    t = pl.program_id(0) % tiles_per_seq
    hl = POOL_HALO
    ext = hl + tm
    pos = (t * tm + 1 + lax.broadcasted_iota(jnp.int32, (tm, 1), 0)).astype(F32)

    @pl.when(t == 0)
    def _():
        halo[...] = jnp.zeros_like(halo)

    for g, w in enumerate(POOL_WINDOWS):
        cols = slice(g * cg, (g + 1) * cg)
        u = u_ref[:, cols].astype(F32)
        buf_a[0:hl, :] = halo[:, cols]
        buf_a[hl:ext, :] = u
        halo[:, cols] = u[tm - hl:, :]
        src, dst, lo, shift = buf_a, buf_b, 0, 1
        while shift < w:
            lo += SUBLANES
            cur = src[lo:ext, :]
            prev = src[lo - SUBLANES:ext - SUBLANES, :]
            dst[lo:ext, :] = cur + (_shift_rows(cur, prev, shift) if shift < SUBLANES else prev)
            src, dst, shift = dst, src, shift * 2
        inv_cnt = 1.0 / jnp.minimum(pos, float(w))
        d = (src[hl:ext, :] * inv_cnt - u).astype(BF16)
        y = jnp.dot(d, maps_ref[g], preferred_element_type=F32)
        o_ref[:, cols] = (y * scale_ref[:, cols] * _silu(z_ref[:, cols].astype(F32))).astype(o_ref.dtype)


def _pool_mix(proj, maps4, layer, scale, seq, bw, tm=256):
    n = proj.shape[0]
    tm = min(tm, seq)
    ng, cg = maps4.shape[1], maps4.shape[2]
    kern = functools.partial(_pool_kernel, tm=tm, tiles_per_seq=seq // tm, cg=cg)
    est = 2 * (3 * tm * bw * 2 + ng * cg * cg * 2) + 2 * (POOL_HALO + tm) * cg * 4 + POOL_HALO * bw * 4 + 4 * tm * cg * 4
    return pl.pallas_call(
        kern,
        grid=(n // tm,),
        in_specs=[
            pl.BlockSpec((tm, bw), lambda i: (i, 0)),
            pl.BlockSpec((tm, bw), lambda i: (i, 1)),
            pl.BlockSpec((None, ng, cg, cg), lambda i: (layer, 0, 0, 0)),
            pl.BlockSpec((1, bw), lambda i: (0, 0)),
        ],
        out_specs=pl.BlockSpec((tm, bw), lambda i: (i, 0)),
        out_shape=jax.ShapeDtypeStruct((n, bw), BF16),
        scratch_shapes=[
            pltpu.VMEM((POOL_HALO + tm, cg), F32),
            pltpu.VMEM((POOL_HALO + tm, cg), F32),
            pltpu.VMEM((POOL_HALO, bw), F32),
        ],
        compiler_params=pltpu.CompilerParams(dimension_semantics=("arbitrary",), vmem_limit_bytes=_vmem_limit(est)),
        name="pool_mix",
    )(proj, proj, maps4, scale.reshape(1, bw).astype(F32))


def _xattn_kernel(q_ref, z_ref, k_ref, v_ref, o_ref, *, hd):
    scale = hd ** -0.5
    for h in range(XA_HEADS):
        cols = slice(h * hd, (h + 1) * hd)
        s = lax.dot_general(q_ref[:, cols], k_ref[:, cols], (((1,), (1,)), ((), ())),
                            preferred_element_type=F32) * scale
        e = jnp.exp(s - jnp.max(s, axis=-1, keepdims=True))
        p = e * (1.0 / jnp.sum(e, axis=-1, keepdims=True))
        o = jnp.dot(p.astype(BF16), v_ref[:, cols], preferred_element_type=F32)
        o_ref[:, cols] = (o * _silu(z_ref[:, cols].astype(F32))).astype(o_ref.dtype)


def _xattn(proj, kv, layer, seq, mem_tokens, q_blk, z_blk, xw, tm=512):
    n = proj.shape[0]
    tm = min(tm, seq)
    tps = seq // tm
    kern = functools.partial(_xattn_kernel, hd=xw // XA_HEADS)
    est = 2 * (3 * tm * xw * 2 + 2 * mem_tokens * xw * 2) + 6 * tm * mem_tokens * 4
    return pl.pallas_call(
        kern,
        grid=(n // tm,),
        in_specs=[
            pl.BlockSpec((tm, xw), lambda i: (i, q_blk)),
            pl.BlockSpec((tm, xw), lambda i: (i, z_blk)),
            pl.BlockSpec((mem_tokens, xw), lambda i: (i // tps, 2 * layer)),
            pl.BlockSpec((mem_tokens, xw), lambda i: (i // tps, 2 * layer + 1)),
        ],
        out_specs=pl.BlockSpec((tm, xw), lambda i: (i, 0)),
        out_shape=jax.ShapeDtypeStruct((n, xw), BF16),
        compiler_params=pltpu.CompilerParams(dimension_semantics=("arbitrary",), vmem_limit_bytes=_vmem_limit(est)),
        name="mem_xattn",
    )(proj, proj, kv, kv)


def _out_kernel(br_ref, mo_ref, w_ref, h_ref, g_ref, *out_refs, bw, final):
    acc = jnp.dot(br_ref[...], w_ref[0:bw, :], preferred_element_type=F32)
    acc = acc + jnp.dot(mo_ref[...], w_ref[bw:, :], preferred_element_type=F32)
    hn = h_ref[...] + acc
    y = hn * lax.rsqrt(jnp.mean(hn * hn, axis=-1, keepdims=True) + EPS) * g_ref[...]
    if final:
        out_refs[0][...] = y
    else:
        out_refs[0][...] = hn
        out_refs[1][...] = y.astype(out_refs[1].dtype)


def _out_proj(branch, memo, w_out3, layer, h, g_next, final, tm=256):
    n, bw = branch.shape
    xw = memo.shape[1]
    mw, d = w_out3.shape[1], w_out3.shape[2]
    tm = min(tm, n)
    kern = functools.partial(_out_kernel, bw=bw, final=final)
    row = lambda i: (i, 0)
    if final:
        out_specs = [pl.BlockSpec((tm, d), row)]
        out_shape = [jax.ShapeDtypeStruct((n, d), F32)]
    else:
        out_specs = [pl.BlockSpec((tm, d), row), pl.BlockSpec((tm, d), row)]
        out_shape = [jax.ShapeDtypeStruct((n, d), F32), jax.ShapeDtypeStruct((n, d), BF16)]
    est = mw * d * 2 + 2 * (tm * (bw + xw) * 2 + tm * d * 4 * 2 + tm * d * 2) + 3 * tm * d * 4
    return pl.pallas_call(
        kern,
        grid=(n // tm,),
        in_specs=[
            pl.BlockSpec((tm, bw), row),
            pl.BlockSpec((tm, xw), row),
            pl.BlockSpec((None, mw, d), lambda i: (layer, 0, 0), pipeline_mode=pl.Buffered(1)),
            pl.BlockSpec((tm, d), row),
            pl.BlockSpec((1, d), lambda i: (0, 0)),
        ],
        out_specs=out_specs,
        out_shape=out_shape,
        compiler_params=pltpu.CompilerParams(dimension_semantics=("arbitrary",), vmem_limit_bytes=_vmem_limit(est)),
        name="out_proj",
    )(branch, memo, w_out3, h, g_next.reshape(1, d).astype(F32))


def _odd_blocks(x, b):
    return jnp.concatenate([x[i:i + b] for i in range(b, x.shape[0], 2 * b)], axis=0)


def _odd_scatter(y, b):
    zero = jnp.zeros((b, y.shape[1]), y.dtype)
    parts = []
    for i in range(0, y.shape[0], b):
        parts += [zero, y[i:i + b]]
    return jnp.concatenate(parts, axis=0)


def _odd_merge(x, y, b):
    parts = []
    for m, i in enumerate(range(0, x.shape[0], 2 * b)):
        parts += [x[i:i + b], y[m * b:(m + 1) * b]]
    return jnp.concatenate(parts, axis=0)


def _delta_kernel(qkv_ref, z_ref, ba_ref, cw_ref, alog_ref, dtb_ref, ng_ref, o_ref,
                  halo, ext, state, msk_s, *, c, n_qk, strip):
    dk = DN_HEAD_DIM
    rep = 2
    kw = n_qk * dk
    hl = SUBLANES
    half = c // 2
    levels = []
    b = DELTA_BASE_BLOCK
    while b < c:
        levels.append(b)
        b *= 2
    row = lax.broadcasted_iota(jnp.int32, (c, c), 0)
    col = lax.broadcasted_iota(jnp.int32, (c, c), 1)
    eye = (row == col).astype(F32)
    not_causal = jnp.where(row >= col, 0.0, MASKED)

    @pl.when(pl.program_id(1) == 0)
    def _():
        halo[...] = jnp.zeros_like(halo)
        state[...] = jnp.zeros_like(state)
        sh0 = DELTA_BASE_BLOCK.bit_length() - 1
        msk_s[0:c, :] = jnp.where(((row >> sh0) == (col >> sh0)) & (row > col), -1.0, 0.0)
        rr = lax.broadcasted_iota(jnp.int32, (half, c), 0)
        cc = lax.broadcasted_iota(jnp.int32, (half, c), 1)
        for lvl, blk in enumerate(levels):
            sh = blk.bit_length() - 1
            msk_s[c + lvl * half:c + (lvl + 1) * half, :] = ((cc >> sh) == 2 * (rr >> sh)).astype(F32)

    beta = jax.nn.sigmoid(ba_ref[:, 0:LANES])
    a_in = ba_ref[:, LANES:2 * LANES] + dtb_ref[...]
    softplus = jnp.maximum(a_in, 0.0) + jnp.log1p(jnp.exp(-jnp.abs(a_in)))
    g = -jnp.exp(alog_ref[...]) * softplus
    gc = jnp.dot((row >= col).astype(F32), g, precision=lax.Precision.HIGHEST, preferred_element_type=F32)
    eg = jnp.exp(gc)
    egl = jnp.exp(gc[c - 1:c, :] - gc)
    gct = gc.T
    eg_last = jnp.exp(jnp.broadcast_to(gct[:, c - 1:c], (LANES, LANES)))
    bscale = beta * eg
    g2 = gc * LOG2E
    g2t = gct * LOG2E

    for s0 in range(0, qkv_ref.shape[1], strip):
        cs = slice(s0, s0 + strip)
        x = qkv_ref[:, cs].astype(F32)
        ext[0:hl, cs] = halo[:, cs]
        ext[hl:hl + c, cs] = x
        halo[:, cs] = x[c - hl:, :]

    def conv_act(c0):
        cs = slice(c0, c0 + dk)
        cur = ext[hl:hl + c, cs]
        prev = ext[0:c, cs]
        half_w = 0.5 * cw_ref[:, cs]
        y = cur * half_w[DN_CONV_WIDTH - 1:DN_CONV_WIDTH]
        for j in range(1, DN_CONV_WIDTH):
            y = y + _shift_rows(cur, prev, j) * half_w[DN_CONV_WIDTH - 1 - j:DN_CONV_WIDTH - j]
        return _silu_of_half(y)

    def l2n(a, scale=1.0):
        return a * (lax.rsqrt(jnp.sum(a * a, axis=-1, keepdims=True) + EPS) * scale)

    def dot(a, b):
        return jnp.dot(a, b, preferred_element_type=F32)

    ng = ng_ref[...]
    nt = (((1,), (1,)), ((), ()))
    for j0 in range(0, n_qk, DELTA_GROUP_QK):
        js = list(range(j0, min(j0 + DELTA_GROUP_QK, n_qk)))
        hvs = [rep * j + r for j in js for r in range(rep)]
        q = {j: l2n(conv_act(j * dk), dk ** -0.5) for j in js}
        k = {j: l2n(conv_act(kw + j * dk)) for j in js}
        kb = {j: k[j].astype(BF16) for j in js}
        p = {j: lax.dot_general(jnp.concatenate([q[j].astype(BF16), kb[j]], axis=0), kb[j], nt,
                                preferred_element_type=F32) for j in js}
        dm, a_raw = {}, {}
        for hv in hvs:
            dm[hv] = jnp.exp2(g2[:, hv:hv + 1] - g2t[hv:hv + 1, :] + not_causal)
            a_raw[hv] = p[hv // rep][c:2 * c] * (dm[hv] * beta[:, hv:hv + 1])
        n1 = {hv: a_raw[hv] * msk_s[0:c, :] for hv in hvs}
        t = {hv: eye + n1[hv] for hv in hvs}
        pw = {hv: n1[hv].astype(BF16) for hv in hvs}
        sq = 1
        while 2 * sq < DELTA_BASE_BLOCK:
            pw = {hv: dot(pw[hv], pw[hv]).astype(BF16) for hv in hvs}
            t = {hv: t[hv] + dot(t[hv].astype(BF16), pw[hv]) for hv in hvs}
            sq *= 2
        for lvl, blk in enumerate(levels):
            lm = msk_s[c + lvl * half:c + (lvl + 1) * half, :]
            m = {hv: dot((_odd_blocks(a_raw[hv], blk) * lm).astype(BF16), t[hv].astype(BF16)) for hv in hvs}
            t = {hv: _odd_merge(t[hv], _odd_blocks(t[hv], blk)
                                - dot(_odd_blocks(t[hv], blk).astype(BF16), _odd_scatter(m[hv], blk).astype(BF16)),
                                blk) for hv in hvs}
        uw = {}
        for hv in hvs:
            v = conv_act(2 * kw + hv * dk)
            rhs = jnp.concatenate([(v * beta[:, hv:hv + 1]).astype(BF16),
                                   (k[hv // rep] * bscale[:, hv:hv + 1]).astype(BF16)], axis=1)
            uw[hv] = dot(t[hv].astype(BF16), rhs)
        st = {hv: state[hv] for hv in hvs}
        wq = {hv: dot(jnp.concatenate([uw[hv][:, dk:2 * dk].astype(BF16),
                                       (q[hv // rep] * eg[:, hv:hv + 1]).astype(BF16)], axis=0),
                      st[hv].astype(BF16)) for hv in hvs}
        v_new = {hv: (uw[hv][:, 0:dk] - wq[hv][0:c]).astype(BF16) for hv in hvs}
        o = {hv: wq[hv][c:2 * c] + dot((p[hv // rep][0:c] * dm[hv]).astype(BF16), v_new[hv]) for hv in hvs}
        for hv in hvs:
            kdt = (k[hv // rep] * egl[:, hv:hv + 1]).T.astype(BF16)
            state[hv] = st[hv] * eg_last[hv:hv + 1, :] + dot(kdt, v_new[hv])
        for hv in hvs:
            cols = slice(hv * dk, (hv + 1) * dk)
            on = o[hv] * lax.rsqrt(jnp.mean(o[hv] * o[hv], axis=-1, keepdims=True) + EPS) * ng
            o_ref[:, cols] = (on * _silu(z_ref[:, cols].astype(F32))).astype(o_ref.dtype)


def _delta_mix(proj, ba, conv_w, a_log, dt_bias, norm_g, batch, seq, bw):
    n = proj.shape[0]
    c = min(DELTA_CHUNK, seq)
    dk = DN_HEAD_DIM
    n_v = bw // dk
    n_qk = n_v // 2
    cw = 2 * n_qk * dk + bw
    nch = seq // c
    strip = 512
    n_lvl = (c // DELTA_BASE_BLOCK).bit_length() - 1
    kern = functools.partial(_delta_kernel, c=c, n_qk=n_qk, strip=strip)

    def pad_lanes(v):
        return jnp.zeros((1, LANES), F32).at[0, :v.shape[0]].set(v.astype(F32))

    rowblk = lambda b, t: (b * nch + t, 0)
    est = (2 * (c * cw * 2 + 2 * c * bw * 2 + c * 2 * LANES * 4) + (2 * SUBLANES + c) * cw * 4
           + n_v * dk * dk * 4 + (2 + n_lvl) * c * c * 2 + 16 * 1024 * 1024)
    return pl.pallas_call(
        kern,
        grid=(batch, nch),
        in_specs=[
            pl.BlockSpec((c, cw), rowblk),
            pl.BlockSpec((c, bw), lambda b, t: (b * nch + t, cw // bw)),
            pl.BlockSpec((c, 2 * LANES), rowblk),
            pl.BlockSpec((DN_CONV_WIDTH, cw), lambda b, t: (0, 0)),
            pl.BlockSpec((1, LANES), lambda b, t: (0, 0)),
            pl.BlockSpec((1, LANES), lambda b, t: (0, 0)),
            pl.BlockSpec((1, dk), lambda b, t: (0, 0)),
        ],
        out_specs=pl.BlockSpec((c, bw), rowblk),
        out_shape=jax.ShapeDtypeStruct((n, bw), BF16),
        scratch_shapes=[
            pltpu.VMEM((SUBLANES, cw), F32),
            pltpu.VMEM((SUBLANES + c, cw), F32),
            pltpu.VMEM((n_v, dk, dk), F32),
            pltpu.VMEM((c + n_lvl * (c // 2), c), F32),
        ],
        compiler_params=pltpu.CompilerParams(dimension_semantics=("arbitrary", "arbitrary"),
                                             vmem_limit_bytes=_vmem_limit(est)),
        name="delta_mix",
    )(proj, proj, ba, conv_w.astype(F32), pad_lanes(a_log), pad_lanes(dt_bias),
      norm_g.reshape(1, dk).astype(F32))


def kernel(x, mem, layer_norm_g, mem_norm_g, final_norm_g, w_in_pool, pool_maps, pool_scale, w_in_delta,
           dn_conv_w, dn_a_log, dn_dt_bias, dn_norm_g, w_mem_kv, w_out):
    batch, seq, d = x.shape
    mem_tokens = mem.shape[1]
    depth = layer_norm_g.shape[0]
    n = batch * seq
    bw = pool_scale.shape[1]
    xw = w_mem_kv.shape[2] // 2
    n_v = dn_a_log.shape[1]
    main_w = w_in_delta.shape[2] - 2 * n_v
    cwid = main_w - (bw + xw) - xw

    w_in_pool_b = w_in_pool.astype(BF16)
    w_in_delta_b = w_in_delta[:, :, :main_w].astype(BF16)
    pool_maps_b = pool_maps.astype(BF16)
    w_mem_kv_b = w_mem_kv.astype(BF16)
    w_out_b = w_out.astype(BF16)

    mem_n = _rmsnorm(mem.reshape(batch * mem_tokens, d), mem_norm_g, BF16)
    kv = _matmul(mem_n, w_mem_kv_b, depth, 2 * xw, BF16, name="mem_kv")

    h = x.reshape(n, d)
    xn = _rmsnorm(h, layer_norm_g[0], BF16)
    for layer in range(depth):
        j = layer // 2
        if layer % 2 == 0:
            proj = _matmul(xn, w_in_pool_b, 1, w_in_pool.shape[2], BF16, layer0=j, name="in_proj_pool")
            branch = _pool_mix(proj, pool_maps_b, j, pool_scale[j], seq, bw)
            z_mem_blk, q_blk = (bw + bw) // xw, (bw + bw + xw) // xw
        else:
            side = jnp.zeros((d, 2 * LANES), BF16)
            side = side.at[:, 0:n_v].set(w_in_delta[j, :, main_w:main_w + n_v].astype(BF16))
            side = side.at[:, LANES:LANES + n_v].set(w_in_delta[j, :, main_w + n_v:].astype(BF16))
            proj, ba = _matmul(xn, w_in_delta_b, 1, main_w, BF16, layer0=j, side_w=side, name="in_proj_delta")
            branch = _delta_mix(proj, ba, dn_conv_w[j], dn_a_log[j], dn_dt_bias[j], dn_norm_g[j], batch, seq, bw)
            z_mem_blk, q_blk = (cwid + bw) // xw, (cwid + bw + xw) // xw
        memo = _xattn(proj, kv, layer, seq, mem_tokens, q_blk, z_mem_blk, xw)
        final = layer == depth - 1
        g_next = final_norm_g if final else layer_norm_g[layer + 1]
        outs = _out_proj(branch, memo, w_out_b, layer, h, g_next, final)
        if final:
            return outs[0].reshape(batch, seq, d)
        h, xn = outs
```

```python
import functools

import jax
import jax.numpy as jnp
from jax import lax
from jax.experimental import pallas as pl
from jax.experimental.pallas import tpu as pltpu

F32 = jnp.float32
BF16 = jnp.bfloat16
EPS = 1e-6
LOG2E = 1.4426950408889634

POOL_WINDOWS = (2, 4, 8, 16)
POOL_HALO = 32
XA_HEADS = 4
DN_HEAD_DIM = 128
DN_CONV_WIDTH = 4
DELTA_CHUNK = 128
DELTA_GROUP_QK = 8
DELTA_BASE_BLOCK = 8
SUBLANES = 8
LANES = 128
MASKED = -1e30

V7X_VMEM_BYTES = 64 * 1024 * 1024


def _vmem_limit(estimate_bytes):
    return int(min(max(estimate_bytes * 5 // 4, 16 * 1024 * 1024), V7X_VMEM_BYTES - 6 * 1024 * 1024))


def _silu_of_half(h):
    return h + h * jnp.tanh(h)


def _silu(x):
    return _silu_of_half(0.5 * x)


def _shift_rows(cur, prev, s):
    r, w = cur.shape
    tiles = (r // SUBLANES, SUBLANES, w)
    wraps = lax.broadcasted_iota(jnp.int32, (1, SUBLANES, w), 1) >= SUBLANES - s
    m = jnp.where(wraps, prev.reshape(tiles), cur.reshape(tiles))
    return pltpu.roll(m, s, axis=1).reshape(r, w)


def _rmsnorm_kernel(x_ref, g_ref, o_ref):
    x = x_ref[...].astype(F32)
    ms = jnp.mean(x * x, axis=-1, keepdims=True)
    o_ref[...] = (x * lax.rsqrt(ms + EPS) * g_ref[...]).astype(o_ref.dtype)


def _rmsnorm(x2d, g, out_dtype, tm=512):
    n, d = x2d.shape
    tm = min(tm, n)
    return pl.pallas_call(
        _rmsnorm_kernel,
        grid=(n // tm,),
        in_specs=[pl.BlockSpec((tm, d), lambda i: (i, 0)), pl.BlockSpec((1, d), lambda i: (0, 0))],
        out_specs=pl.BlockSpec((tm, d), lambda i: (i, 0)),
        out_shape=jax.ShapeDtypeStruct((n, d), out_dtype),
        compiler_params=pltpu.CompilerParams(dimension_semantics=("arbitrary",)),
        name="rmsnorm_cast",
    )(x2d, g.reshape(1, d).astype(F32))


def _mm_kernel(a_ref, w_ref, o_ref):
    o_ref[...] = jnp.dot(a_ref[...], w_ref[...], preferred_element_type=F32).astype(o_ref.dtype)


def _mm_side_kernel(a_ref, w_ref, ws_ref, o_ref, os_ref):
    o_ref[...] = jnp.dot(a_ref[...], w_ref[...], preferred_element_type=F32).astype(o_ref.dtype)

    @pl.when(pl.program_id(1) == 0)
    def _():
        os_ref[...] = jnp.dot(a_ref[...], ws_ref[...], preferred_element_type=F32)


def _matmul(a, w3, n_layers, n_out, out_dtype, layer0=0, side_w=None, tm=1024, tn=2048, name="proj"):
    m, k = a.shape
    tm, tn = min(tm, m), min(tn, n_out)
    npl = n_out // tn
    grid = (m // tm, n_layers * npl)
    a_spec = pl.BlockSpec((tm, k), lambda i, j: (i, 0))
    w_spec = pl.BlockSpec((None, k, tn), lambda i, j: (layer0 + j // npl, 0, j % npl))
    o_spec = pl.BlockSpec((tm, tn), lambda i, j: (i, j))
    n_total = n_layers * n_out
    est = 2 * (tm * k * 2 + k * tn * 2 + tm * tn * 2) + tm * tn * 4
    params = pltpu.CompilerParams(dimension_semantics=("arbitrary", "arbitrary"),
                                  vmem_limit_bytes=_vmem_limit(est + 4 * 1024 * 1024))
    if side_w is None:
        return pl.pallas_call(
            _mm_kernel, grid=grid, in_specs=[a_spec, w_spec], out_specs=o_spec,
            out_shape=jax.ShapeDtypeStruct((m, n_total), out_dtype), compiler_params=params, name=name,
        )(a, w3)
    ns = side_w.shape[1]
    return pl.pallas_call(
        _mm_side_kernel, grid=grid,
        in_specs=[a_spec, w_spec, pl.BlockSpec((k, ns), lambda i, j: (0, 0))],
        out_specs=[o_spec, pl.BlockSpec((tm, ns), lambda i, j: (i, 0))],
        out_shape=[jax.ShapeDtypeStruct((m, n_total), out_dtype), jax.ShapeDtypeStruct((m, ns), F32)],
        compiler_params=params, name=name,
    )(a, w3, side_w)


def _pool_kernel(u_ref, z_ref, maps_ref, scale_ref, o_ref, buf_a, buf_b, halo, *, tm, tiles_per_seq, cg):
    t = pl.program_id(0) % tiles_per_seq
    hl = POOL_HALO
    ext = hl + tm
    pos = (t * tm + 1 + lax.broadcasted_iota(jnp.int32, (tm, 1), 0)).astype(F32)

    @pl.when(t == 0)
    def _():
        halo[...] = jnp.zeros_like(halo)

    for g, w in enumerate(POOL_WINDOWS):
        cols = slice(g * cg, (g + 1) * cg)
        u = u_ref[:, cols].astype(F32)
        buf_a[0:hl, :] = halo[:, cols]
        buf_a[hl:ext, :] = u
        halo[:, cols] = u[tm - hl:, :]
        src, dst, lo, shift = buf_a, buf_b, 0, 1
        while shift < w:
            lo += SUBLANES
            cur = src[lo:ext, :]
            prev = src[lo - SUBLANES:ext - SUBLANES, :]
            dst[lo:ext, :] = cur + (_shift_rows(cur, prev, shift) if shift < SUBLANES else prev)
            src, dst, shift = dst, src, shift * 2
        inv_cnt = 1.0 / jnp.minimum(pos, float(w))
        d = (src[hl:ext, :] * inv_cnt - u).astype(BF16)
        y = jnp.dot(d, maps_ref[g], preferred_element_type=F32)
        o_ref[:, cols] = (y * scale_ref[:, cols] * _silu(z_ref[:, cols].astype(F32))).astype(o_ref.dtype)


def _pool_mix(proj, maps4, layer, scale, seq, bw, tm=256):
    n = proj.shape[0]
    tm = min(tm, seq)
    ng, cg = maps4.shape[1], maps4.shape[2]
    kern = functools.partial(_pool_kernel, tm=tm, tiles_per_seq=seq // tm, cg=cg)
    est = 2 * (3 * tm * bw * 2 + ng * cg * cg * 2) + 2 * (POOL_HALO + tm) * cg * 4 + POOL_HALO * bw * 4 + 4 * tm * cg * 4
    return pl.pallas_call(
        kern,
        grid=(n // tm,),
        in_specs=[
            pl.BlockSpec((tm, bw), lambda i: (i, 0)),
            pl.BlockSpec((tm, bw), lambda i: (i, 1)),
            pl.BlockSpec((None, ng, cg, cg), lambda i: (layer, 0, 0, 0)),
            pl.BlockSpec((1, bw), lambda i: (0, 0)),
        ],
        out_specs=pl.BlockSpec((tm, bw), lambda i: (i, 0)),
        out_shape=jax.ShapeDtypeStruct((n, bw), BF16),
        scratch_shapes=[
            pltpu.VMEM((POOL_HALO + tm, cg), F32),
            pltpu.VMEM((POOL_HALO + tm, cg), F32),
            pltpu.VMEM((POOL_HALO, bw), F32),
        ],
        compiler_params=pltpu.CompilerParams(dimension_semantics=("arbitrary",), vmem_limit_bytes=_vmem_limit(est)),
        name="pool_mix",
    )(proj, proj, maps4, scale.reshape(1, bw).astype(F32))


def _xattn_kernel(q_ref, z_ref, k_ref, v_ref, o_ref, *, hd):
    c = hd ** -0.5 * LOG2E
    heads = [slice(h * hd, (h + 1) * hd) for h in range(XA_HEADS)]
    s = [lax.dot_general(q_ref[:, cols], k_ref[:, cols], (((1,), (1,)), ((), ())), preferred_element_type=F32)
         for cols in heads]
    p = []
    for sh in s:
        e = jnp.exp2((sh - jnp.max(sh, axis=-1, keepdims=True)) * c)
        p.append((e * (1.0 / jnp.sum(e, axis=-1, keepdims=True))).astype(BF16))
    o = [jnp.dot(ph, v_ref[:, cols], preferred_element_type=F32) for ph, cols in zip(p, heads)]
    for oh, cols in zip(o, heads):
        o_ref[:, cols] = (oh * _silu(z_ref[:, cols].astype(F32))).astype(o_ref.dtype)


def _xattn(proj, kv, layer, seq, mem_tokens, q_blk, z_blk, xw, tm=512):
    n = proj.shape[0]
    tm = min(tm, seq)
    tps = seq // tm
    kern = functools.partial(_xattn_kernel, hd=xw // XA_HEADS)
    est = 2 * (3 * tm * xw * 2 + 2 * mem_tokens * xw * 2) + 6 * tm * mem_tokens * 4
    return pl.pallas_call(
        kern,
        grid=(n // tm,),
        in_specs=[
            pl.BlockSpec((tm, xw), lambda i: (i, q_blk)),
            pl.BlockSpec((tm, xw), lambda i: (i, z_blk)),
            pl.BlockSpec((mem_tokens, xw), lambda i: (i // tps, 2 * layer)),
            pl.BlockSpec((mem_tokens, xw), lambda i: (i // tps, 2 * layer + 1)),
        ],
        out_specs=pl.BlockSpec((tm, xw), lambda i: (i, 0)),
        out_shape=jax.ShapeDtypeStruct((n, xw), BF16),
        compiler_params=pltpu.CompilerParams(dimension_semantics=("arbitrary",), vmem_limit_bytes=_vmem_limit(est)),
        name="mem_xattn",
    )(proj, proj, kv, kv)


def _out_kernel(br_ref, mo_ref, w_ref, h_ref, g_ref, *out_refs, bw, final):
    acc = jnp.dot(br_ref[...], w_ref[0:bw, :], preferred_element_type=F32)
    acc = acc + jnp.dot(mo_ref[...], w_ref[bw:, :], preferred_element_type=F32)
    hn = h_ref[...] + acc
    y = hn * lax.rsqrt(jnp.mean(hn * hn, axis=-1, keepdims=True) + EPS) * g_ref[...]
    if final:
        out_refs[0][...] = y
    else:
        out_refs[0][...] = hn
        out_refs[1][...] = y.astype(out_refs[1].dtype)


def _out_proj(branch, memo, w_out3, layer, h, g_next, final, tm=256):
    n, bw = branch.shape
    xw = memo.shape[1]
    mw, d = w_out3.shape[1], w_out3.shape[2]
    tm = min(tm, n)
    kern = functools.partial(_out_kernel, bw=bw, final=final)
    row = lambda i: (i, 0)
    if final:
        out_specs = [pl.BlockSpec((tm, d), row)]
        out_shape = [jax.ShapeDtypeStruct((n, d), F32)]
    else:
        out_specs = [pl.BlockSpec((tm, d), row), pl.BlockSpec((tm, d), row)]
        out_shape = [jax.ShapeDtypeStruct((n, d), F32), jax.ShapeDtypeStruct((n, d), BF16)]
    est = mw * d * 2 + 2 * (tm * (bw + xw) * 2 + tm * d * 4 * 2 + tm * d * 2) + 3 * tm * d * 4
    return pl.pallas_call(
        kern,
        grid=(n // tm,),
        in_specs=[
            pl.BlockSpec((tm, bw), row),
            pl.BlockSpec((tm, xw), row),
            pl.BlockSpec((None, mw, d), lambda i: (layer, 0, 0), pipeline_mode=pl.Buffered(1)),
            pl.BlockSpec((tm, d), row),
            pl.BlockSpec((1, d), lambda i: (0, 0)),
        ],
        out_specs=out_specs,
        out_shape=out_shape,
        compiler_params=pltpu.CompilerParams(dimension_semantics=("arbitrary",), vmem_limit_bytes=_vmem_limit(est)),
        name="out_proj",
    )(branch, memo, w_out3, h, g_next.reshape(1, d).astype(F32))


def _odd_blocks(x, b):
    return jnp.concatenate([x[i:i + b] for i in range(b, x.shape[0], 2 * b)], axis=0)


def _odd_scatter(y, b):
    zero = jnp.zeros((b, y.shape[1]), y.dtype)
    parts = []
    for i in range(0, y.shape[0], b):
        parts += [zero, y[i:i + b]]
    return jnp.concatenate(parts, axis=0)


def _odd_merge(x, y, b):
    parts = []
    for m, i in enumerate(range(0, x.shape[0], 2 * b)):
        parts += [x[i:i + b], y[m * b:(m + 1) * b]]
    return jnp.concatenate(parts, axis=0)


def _delta_kernel(qkv_ref, z_ref, ba_ref, cw_ref, alog_ref, dtb_ref, ng_ref, o_ref,
                  halo, ext, state, msk_s, *, c, n_qk, strip):
    dk = DN_HEAD_DIM
    rep = 2
    kw = n_qk * dk
    hl = SUBLANES
    half = c // 2
    levels = []
    b = DELTA_BASE_BLOCK
    while b < c:
        levels.append(b)
        b *= 2
    row = lax.broadcasted_iota(jnp.int32, (c, c), 0)
    col = lax.broadcasted_iota(jnp.int32, (c, c), 1)
    eye = (row == col).astype(F32)
    not_causal = jnp.where(row >= col, 0.0, MASKED)

    @pl.when(pl.program_id(1) == 0)
    def _():
        halo[...] = jnp.zeros_like(halo)
        state[...] = jnp.zeros_like(state)
        sh0 = DELTA_BASE_BLOCK.bit_length() - 1
        msk_s[0:c, :] = jnp.where(((row >> sh0) == (col >> sh0)) & (row > col), -1.0, 0.0)
        rr = lax.broadcasted_iota(jnp.int32, (half, c), 0)
        cc = lax.broadcasted_iota(jnp.int32, (half, c), 1)
        for lvl, blk in enumerate(levels):
            sh = blk.bit_length() - 1
            msk_s[c + lvl * half:c + (lvl + 1) * half, :] = ((cc >> sh) == 2 * (rr >> sh)).astype(F32)

    beta = jax.nn.sigmoid(ba_ref[:, 0:LANES])
    a_in = ba_ref[:, LANES:2 * LANES] + dtb_ref[...]
    softplus = jnp.maximum(a_in, 0.0) + jnp.log1p(jnp.exp(-jnp.abs(a_in)))
    g = -jnp.exp(alog_ref[...]) * softplus
    gc = jnp.dot((row >= col).astype(F32), g, precision=lax.Precision.HIGHEST, preferred_element_type=F32)
    eg = jnp.exp(gc)
    egl = jnp.exp(gc[c - 1:c, :] - gc)
    gct = gc.T
    eg_last = jnp.exp(jnp.broadcast_to(gct[:, c - 1:c], (LANES, LANES)))
    bscale = beta * eg
    g2 = gc * LOG2E
    g2t = gct * LOG2E

    for s0 in range(0, qkv_ref.shape[1], strip):
        cs = slice(s0, s0 + strip)
        x = qkv_ref[:, cs].astype(F32)
        ext[0:hl, cs] = halo[:, cs]
        ext[hl:hl + c, cs] = x
        halo[:, cs] = x[c - hl:, :]

    def conv_act(c0):
        cs = slice(c0, c0 + dk)
        cur = ext[hl:hl + c, cs]
        prev = ext[0:c, cs]
        half_w = 0.5 * cw_ref[:, cs]
        y = cur * half_w[DN_CONV_WIDTH - 1:DN_CONV_WIDTH]
        for j in range(1, DN_CONV_WIDTH):
            y = y + _shift_rows(cur, prev, j) * half_w[DN_CONV_WIDTH - 1 - j:DN_CONV_WIDTH - j]
        return _silu_of_half(y)

    def l2n(a, scale=1.0):
        return a * (lax.rsqrt(jnp.sum(a * a, axis=-1, keepdims=True) + EPS) * scale)

    def dot(a, b):
        return jnp.dot(a, b, preferred_element_type=F32)

    ng = ng_ref[...]
    nt = (((1,), (1,)), ((), ()))
    for j0 in range(0, n_qk, DELTA_GROUP_QK):
        js = list(range(j0, min(j0 + DELTA_GROUP_QK, n_qk)))
        hvs = [rep * j + r for j in js for r in range(rep)]
        q = {j: l2n(conv_act(j * dk), dk ** -0.5) for j in js}
        k = {j: l2n(conv_act(kw + j * dk)) for j in js}
        kb = {j: k[j].astype(BF16) for j in js}
        p = {j: lax.dot_general(jnp.concatenate([q[j].astype(BF16), kb[j]], axis=0), kb[j], nt,
                                preferred_element_type=F32) for j in js}
        dm, a_raw = {}, {}
        for hv in hvs:
            dm[hv] = jnp.exp2(g2[:, hv:hv + 1] - g2t[hv:hv + 1, :] + not_causal)
            a_raw[hv] = p[hv // rep][c:2 * c] * (dm[hv] * beta[:, hv:hv + 1])
        n1 = {hv: a_raw[hv] * msk_s[0:c, :] for hv in hvs}
        t = {hv: eye + n1[hv] for hv in hvs}
        pw = {hv: n1[hv].astype(BF16) for hv in hvs}
        sq = 1
        while 2 * sq < DELTA_BASE_BLOCK:
            pw = {hv: dot(pw[hv], pw[hv]).astype(BF16) for hv in hvs}
            t = {hv: t[hv] + dot(t[hv].astype(BF16), pw[hv]) for hv in hvs}
            sq *= 2
        for lvl, blk in enumerate(levels):
            lm = msk_s[c + lvl * half:c + (lvl + 1) * half, :]
            m = {hv: dot((_odd_blocks(a_raw[hv], blk) * lm).astype(BF16), t[hv].astype(BF16)) for hv in hvs}
            t = {hv: _odd_merge(t[hv], _odd_blocks(t[hv], blk)
                                - dot(_odd_blocks(t[hv], blk).astype(BF16), _odd_scatter(m[hv], blk).astype(BF16)),
                                blk) for hv in hvs}
        uw = {}
        for hv in hvs:
            v = conv_act(2 * kw + hv * dk)
            rhs = jnp.concatenate([(v * beta[:, hv:hv + 1]).astype(BF16),
                                   (k[hv // rep] * bscale[:, hv:hv + 1]).astype(BF16)], axis=1)
            uw[hv] = dot(t[hv].astype(BF16), rhs)
        st = {hv: state[hv] for hv in hvs}
        wq = {hv: dot(jnp.concatenate([uw[hv][:, dk:2 * dk].astype(BF16),
                                       (q[hv // rep] * eg[:, hv:hv + 1]).astype(BF16)], axis=0),
                      st[hv].astype(BF16)) for hv in hvs}
        v_new = {hv: (uw[hv][:, 0:dk] - wq[hv][0:c]).astype(BF16) for hv in hvs}
        o = {hv: wq[hv][c:2 * c] + dot((p[hv // rep][0:c] * dm[hv]).astype(BF16), v_new[hv]) for hv in hvs}
        for hv in hvs:
            kdt = (k[hv // rep] * egl[:, hv:hv + 1]).T.astype(BF16)
            state[hv] = st[hv] * eg_last[hv:hv + 1, :] + dot(kdt, v_new[hv])
        for hv in hvs:
            cols = slice(hv * dk, (hv + 1) * dk)
            on = o[hv] * lax.rsqrt(jnp.mean(o[hv] * o[hv], axis=-1, keepdims=True) + EPS) * ng
            o_ref[:, cols] = (on * _silu(z_ref[:, cols].astype(F32))).astype(o_ref.dtype)


def _delta_mix(proj, ba, conv_w, a_log, dt_bias, norm_g, batch, seq, bw):
    n = proj.shape[0]
    c = min(DELTA_CHUNK, seq)
    dk = DN_HEAD_DIM
    n_v = bw // dk
    n_qk = n_v // 2
    cw = 2 * n_qk * dk + bw
    nch = seq // c
    strip = 512
    n_lvl = (c // DELTA_BASE_BLOCK).bit_length() - 1
    kern = functools.partial(_delta_kernel, c=c, n_qk=n_qk, strip=strip)

    def pad_lanes(v):
        return jnp.zeros((1, LANES), F32).at[0, :v.shape[0]].set(v.astype(F32))

    rowblk = lambda b, t: (b * nch + t, 0)
    est = (2 * (c * cw * 2 + 2 * c * bw * 2 + c * 2 * LANES * 4) + (2 * SUBLANES + c) * cw * 4
           + n_v * dk * dk * 4 + (2 + n_lvl) * c * c * 2 + 16 * 1024 * 1024)
    return pl.pallas_call(
        kern,
        grid=(batch, nch),
        in_specs=[
            pl.BlockSpec((c, cw), rowblk),
            pl.BlockSpec((c, bw), lambda b, t: (b * nch + t, cw // bw)),
            pl.BlockSpec((c, 2 * LANES), rowblk),
            pl.BlockSpec((DN_CONV_WIDTH, cw), lambda b, t: (0, 0)),
            pl.BlockSpec((1, LANES), lambda b, t: (0, 0)),
            pl.BlockSpec((1, LANES), lambda b, t: (0, 0)),
            pl.BlockSpec((1, dk), lambda b, t: (0, 0)),
        ],
        out_specs=pl.BlockSpec((c, bw), rowblk),
        out_shape=jax.ShapeDtypeStruct((n, bw), BF16),
        scratch_shapes=[
            pltpu.VMEM((SUBLANES, cw), F32),
            pltpu.VMEM((SUBLANES + c, cw), F32),
            pltpu.VMEM((n_v, dk, dk), F32),
            pltpu.VMEM((c + n_lvl * (c // 2), c), F32),
        ],
        compiler_params=pltpu.CompilerParams(dimension_semantics=("arbitrary", "arbitrary"),
                                             vmem_limit_bytes=_vmem_limit(est)),
        name="delta_mix",
    )(proj, proj, ba, conv_w.astype(F32), pad_lanes(a_log), pad_lanes(dt_bias),
      norm_g.reshape(1, dk).astype(F32))


def kernel(x, mem, layer_norm_g, mem_norm_g, final_norm_g, w_in_pool, pool_maps, pool_scale, w_in_delta,
           dn_conv_w, dn_a_log, dn_dt_bias, dn_norm_g, w_mem_kv, w_out):
    batch, seq, d = x.shape
    mem_tokens = mem.shape[1]
    depth = layer_norm_g.shape[0]
    n = batch * seq
    bw = pool_scale.shape[1]
    xw = w_mem_kv.shape[2] // 2
    n_v = dn_a_log.shape[1]
    main_w = w_in_delta.shape[2] - 2 * n_v
    cwid = main_w - (bw + xw) - xw

    w_in_pool_b = w_in_pool.astype(BF16)
    w_in_delta_b = w_in_delta.astype(BF16)
    pool_maps_b = pool_maps.astype(BF16)
    w_mem_kv_b = w_mem_kv.astype(BF16)
    w_out_b = w_out.astype(BF16)

    mem_n = _rmsnorm(mem.reshape(batch * mem_tokens, d), mem_norm_g, BF16)
    kv = _matmul(mem_n, w_mem_kv_b, depth, 2 * xw, BF16, name="mem_kv")

    h = x.reshape(n, d)
    xn = _rmsnorm(h, layer_norm_g[0], BF16)
    for layer in range(depth):
        j = layer // 2
        if layer % 2 == 0:
            proj = _matmul(xn, w_in_pool_b, 1, w_in_pool.shape[2], BF16, layer0=j, name="in_proj_pool")
            branch = _pool_mix(proj, pool_maps_b, j, pool_scale[j], seq, bw)
            z_mem_blk, q_blk = (bw + bw) // xw, (bw + bw + xw) // xw
        else:
            side = jnp.zeros((d, 2 * LANES), BF16)
            side = side.at[:, 0:n_v].set(w_in_delta[j, :, main_w:main_w + n_v].astype(BF16))
            side = side.at[:, LANES:LANES + n_v].set(w_in_delta[j, :, main_w + n_v:].astype(BF16))
            proj, ba = _matmul(xn, w_in_delta_b, 1, main_w, BF16, layer0=j, side_w=side, name="in_proj_delta")
            branch = _delta_mix(proj, ba, dn_conv_w[j], dn_a_log[j], dn_dt_bias[j], dn_norm_g[j], batch, seq, bw)
            z_mem_blk, q_blk = (cwid + bw) // xw, (cwid + bw + xw) // xw
        memo = _xattn(proj, kv, layer, seq, mem_tokens, q_blk, z_mem_blk, xw)
        final = layer == depth - 1
        g_next = final_norm_g if final else layer_norm_g[layer + 1]
        outs = _out_proj(branch, memo, w_out_b, layer, h, g_next, final)
        if final:
            return outs[0].reshape(batch, seq, d)
        h, xn = outs
```

```python
import functools

import jax
import jax.numpy as jnp
from jax import lax
from jax.experimental import pallas as pl
from jax.experimental.pallas import tpu as pltpu

F32 = jnp.float32
BF16 = jnp.bfloat16
EPS = 1e-6
LOG2E = 1.4426950408889634

POOL_WINDOWS = (2, 4, 8, 16)
POOL_HALO = 32
XA_HEADS = 4
DN_HEAD_DIM = 128
DN_CONV_WIDTH = 4
DELTA_CHUNK = 128
DELTA_GROUP_QK = 8
DELTA_BASE_BLOCK = 8
SUBLANES = 8
LANES = 128
MASKED = -1e30

V7X_VMEM_BYTES = 64 * 1024 * 1024


def _vmem_limit(estimate_bytes):
    return int(min(max(estimate_bytes * 5 // 4, 16 * 1024 * 1024), V7X_VMEM_BYTES - 6 * 1024 * 1024))


def _silu_of_half(h):
    return h + h * jnp.tanh(h)


def _silu(x):
    return _silu_of_half(0.5 * x)


def _shift_rows(cur, prev, s):
    r, w = cur.shape
    tiles = (r // SUBLANES, SUBLANES, w)
    wraps = lax.broadcasted_iota(jnp.int32, (1, SUBLANES, w), 1) >= SUBLANES - s
    m = jnp.where(wraps, prev.reshape(tiles), cur.reshape(tiles))
    return pltpu.roll(m, s, axis=1).reshape(r, w)


def _rmsnorm_kernel(x_ref, g_ref, o_ref):
    x = x_ref[...].astype(F32)
    ms = jnp.mean(x * x, axis=-1, keepdims=True)
    o_ref[...] = (x * lax.rsqrt(ms + EPS) * g_ref[...]).astype(o_ref.dtype)


def _rmsnorm(x2d, g, out_dtype, tm=512):
    n, d = x2d.shape
    tm = min(tm, n)
    return pl.pallas_call(
        _rmsnorm_kernel,
        grid=(n // tm,),
        in_specs=[pl.BlockSpec((tm, d), lambda i: (i, 0)), pl.BlockSpec((1, d), lambda i: (0, 0))],
        out_specs=pl.BlockSpec((tm, d), lambda i: (i, 0)),
        out_shape=jax.ShapeDtypeStruct((n, d), out_dtype),
        compiler_params=pltpu.CompilerParams(dimension_semantics=("arbitrary",)),
        name="rmsnorm_cast",
    )(x2d, g.reshape(1, d).astype(F32))


def _mm_kernel(a_ref, w_ref, o_ref):
    o_ref[...] = jnp.dot(a_ref[...], w_ref[...], preferred_element_type=F32).astype(o_ref.dtype)


def _mm_side_kernel(a_ref, w_ref, ws_ref, o_ref, os_ref):
    o_ref[...] = jnp.dot(a_ref[...], w_ref[...], preferred_element_type=F32).astype(o_ref.dtype)

    @pl.when(pl.program_id(1) == 0)
    def _():
        os_ref[...] = jnp.dot(a_ref[...], ws_ref[...], preferred_element_type=F32)


def _matmul(a, w3, n_layers, n_out, out_dtype, layer0=0, side_w=None, tm=1024, tn=2048, name="proj"):
    m, k = a.shape
    tm, tn = min(tm, m), min(tn, n_out)
    npl = n_out // tn
    grid = (m // tm, n_layers * npl)
    a_spec = pl.BlockSpec((tm, k), lambda i, j: (i, 0))
    w_spec = pl.BlockSpec((None, k, tn), lambda i, j: (layer0 + j // npl, 0, j % npl))
    o_spec = pl.BlockSpec((tm, tn), lambda i, j: (i, j))
    n_total = n_layers * n_out
    est = 2 * (tm * k * 2 + k * tn * 2 + tm * tn * 2) + tm * tn * 4
    params = pltpu.CompilerParams(dimension_semantics=("arbitrary", "arbitrary"),
                                  vmem_limit_bytes=_vmem_limit(est + 4 * 1024 * 1024))
    if side_w is None:
        return pl.pallas_call(
            _mm_kernel, grid=grid, in_specs=[a_spec, w_spec], out_specs=o_spec,
            out_shape=jax.ShapeDtypeStruct((m, n_total), out_dtype), compiler_params=params, name=name,
        )(a, w3)
    ns = side_w.shape[1]
    return pl.pallas_call(
        _mm_side_kernel, grid=grid,
        in_specs=[a_spec, w_spec, pl.BlockSpec((k, ns), lambda i, j: (0, 0))],
        out_specs=[o_spec, pl.BlockSpec((tm, ns), lambda i, j: (i, 0))],
        out_shape=[jax.ShapeDtypeStruct((m, n_total), out_dtype), jax.ShapeDtypeStruct((m, ns), F32)],
        compiler_params=params, name=name,
    )(a, w3, side_w)


def _pool_kernel(u_ref, z_ref, maps_ref, scale_ref, o_ref, buf_a, buf_b, halo, *, tm, tiles_per_seq, cg):
    t = pl.program_id(0) % tiles_per_seq
    hl = POOL_HALO
    ext = hl + tm
    pos = (t * tm + 1 + lax.broadcasted_iota(jnp.int32, (tm, 1), 0)).astype(F32)

    @pl.when(t == 0)
    def _():
        halo[...] = jnp.zeros_like(halo)

    for g, w in enumerate(POOL_WINDOWS):
        cols = slice(g * cg, (g + 1) * cg)
        u = u_ref[:, cols].astype(F32)
        buf_a[0:hl, :] = halo[:, cols]
        buf_a[hl:ext, :] = u
        halo[:, cols] = u[tm - hl:, :]
        src, dst, lo, shift = buf_a, buf_b, 0, 1
        while shift < w:
            lo += SUBLANES
            cur = src[lo:ext, :]
            prev = src[lo - SUBLANES:ext - SUBLANES, :]
            dst[lo:ext, :] = cur + (_shift_rows(cur, prev, shift) if shift < SUBLANES else prev)
            src, dst, shift = dst, src, shift * 2
        inv_cnt = 1.0 / jnp.minimum(pos, float(w))
        d = (src[hl:ext, :] * inv_cnt - u).astype(BF16)
        y = jnp.dot(d, maps_ref[g], preferred_element_type=F32)
        o_ref[:, cols] = (y * scale_ref[:, cols] * _silu(z_ref[:, cols].astype(F32))).astype(o_ref.dtype)


def _pool_mix(proj, maps4, layer, scale, seq, bw, tm=256):
    n = proj.shape[0]
    tm = min(tm, seq)
    ng, cg = maps4.shape[1], maps4.shape[2]
    kern = functools.partial(_pool_kernel, tm=tm, tiles_per_seq=seq // tm, cg=cg)
    est = 2 * (3 * tm * bw * 2 + ng * cg * cg * 2) + 2 * (POOL_HALO + tm) * cg * 4 + POOL_HALO * bw * 4 + 4 * tm * cg * 4
    return pl.pallas_call(
        kern,
        grid=(n // tm,),
        in_specs=[
            pl.BlockSpec((tm, bw), lambda i: (i, 0)),
            pl.BlockSpec((tm, bw), lambda i: (i, 1)),
            pl.BlockSpec((None, ng, cg, cg), lambda i: (layer, 0, 0, 0)),
            pl.BlockSpec((1, bw), lambda i: (0, 0)),
        ],
        out_specs=pl.BlockSpec((tm, bw), lambda i: (i, 0)),
        out_shape=jax.ShapeDtypeStruct((n, bw), BF16),
        scratch_shapes=[
            pltpu.VMEM((POOL_HALO + tm, cg), F32),
            pltpu.VMEM((POOL_HALO + tm, cg), F32),
            pltpu.VMEM((POOL_HALO, bw), F32),
        ],
        compiler_params=pltpu.CompilerParams(dimension_semantics=("arbitrary",), vmem_limit_bytes=_vmem_limit(est)),
        name="pool_mix",
    )(proj, proj, maps4, scale.reshape(1, bw).astype(F32))


def _mem_out_kernel(br_ref, q_ref, z_ref, k_ref, v_ref, w_ref, h_ref, g_ref, *out_refs, bw, hd, final):
    c = hd ** -0.5 * LOG2E
    heads = [slice(h * hd, (h + 1) * hd) for h in range(XA_HEADS)]
    s = [lax.dot_general(q_ref[:, cols], k_ref[:, cols], (((1,), (1,)), ((), ())), preferred_element_type=F32)
         for cols in heads]
    acc = jnp.dot(br_ref[...], w_ref[0:bw, :], preferred_element_type=F32)
    p = []
    for sh in s:
        e = jnp.exp2((sh - jnp.max(sh, axis=-1, keepdims=True)) * c)
        p.append((e * (1.0 / jnp.sum(e, axis=-1, keepdims=True))).astype(BF16))
    o = [jnp.dot(ph, v_ref[:, cols], preferred_element_type=F32) for ph, cols in zip(p, heads)]
    memo = jnp.concatenate([(oh * _silu(z_ref[:, cols].astype(F32))).astype(BF16) for oh, cols in zip(o, heads)],
                           axis=1)
    acc = acc + jnp.dot(memo, w_ref[bw:, :], preferred_element_type=F32)
    hn = h_ref[...] + acc
    y = hn * lax.rsqrt(jnp.mean(hn * hn, axis=-1, keepdims=True) + EPS) * g_ref[...]
    if final:
        out_refs[0][...] = y
    else:
        out_refs[0][...] = hn
        out_refs[1][...] = y.astype(out_refs[1].dtype)


def _mem_out(branch, proj, kv, w_out3, layer, h, g_next, final, seq, mem_tokens, q_blk, z_blk, xw, tm=256):
    n, bw = branch.shape
    mw, d = w_out3.shape[1], w_out3.shape[2]
    tm = min(tm, seq)
    tps = seq // tm
    kern = functools.partial(_mem_out_kernel, bw=bw, hd=xw // XA_HEADS, final=final)
    row = lambda i: (i, 0)
    if final:
        out_specs = [pl.BlockSpec((tm, d), row)]
        out_shape = [jax.ShapeDtypeStruct((n, d), F32)]
    else:
        out_specs = [pl.BlockSpec((tm, d), row), pl.BlockSpec((tm, d), row)]
        out_shape = [jax.ShapeDtypeStruct((n, d), F32), jax.ShapeDtypeStruct((n, d), BF16)]
    est = (mw * d * 2 + 2 * (tm * (bw + 2 * xw) * 2 + 2 * mem_tokens * xw * 2 + tm * d * 4 * 2 + tm * d * 2)
           + 3 * tm * d * 4 + 6 * tm * mem_tokens * 4)
    return pl.pallas_call(
        kern,
        grid=(n // tm,),
        in_specs=[
            pl.BlockSpec((tm, bw), row),
            pl.BlockSpec((tm, xw), lambda i: (i, q_blk)),
            pl.BlockSpec((tm, xw), lambda i: (i, z_blk)),
            pl.BlockSpec((mem_tokens, xw), lambda i: (i // tps, 2 * layer)),
            pl.BlockSpec((mem_tokens, xw), lambda i: (i // tps, 2 * layer + 1)),
            pl.BlockSpec((None, mw, d), lambda i: (layer, 0, 0), pipeline_mode=pl.Buffered(1)),
            pl.BlockSpec((tm, d), row),
            pl.BlockSpec((1, d), lambda i: (0, 0)),
        ],
        out_specs=out_specs,
        out_shape=out_shape,
        compiler_params=pltpu.CompilerParams(dimension_semantics=("arbitrary",), vmem_limit_bytes=_vmem_limit(est)),
        name="mem_out_proj",
    )(branch, proj, proj, kv, kv, w_out3, h, g_next.reshape(1, d).astype(F32))


def _odd_blocks(x, b):
    return jnp.concatenate([x[i:i + b] for i in range(b, x.shape[0], 2 * b)], axis=0)


def _odd_scatter(y, b):
    zero = jnp.zeros((b, y.shape[1]), y.dtype)
    parts = []
    for i in range(0, y.shape[0], b):
        parts += [zero, y[i:i + b]]
    return jnp.concatenate(parts, axis=0)


def _odd_merge(x, y, b):
    parts = []
    for m, i in enumerate(range(0, x.shape[0], 2 * b)):
        parts += [x[i:i + b], y[m * b:(m + 1) * b]]
    return jnp.concatenate(parts, axis=0)


def _delta_kernel(qkv_ref, z_ref, ba_ref, cw_ref, alog_ref, dtb_ref, ng_ref, o_ref,
                  halo, ext, state, msk_s, *, c, n_qk, strip):
    dk = DN_HEAD_DIM
    rep = 2
    kw = n_qk * dk
    hl = SUBLANES
    half = c // 2
    levels = []
    b = DELTA_BASE_BLOCK
    while b < c:
        levels.append(b)
        b *= 2
    row = lax.broadcasted_iota(jnp.int32, (c, c), 0)
    col = lax.broadcasted_iota(jnp.int32, (c, c), 1)
    eye = (row == col).astype(F32)
    not_causal = jnp.where(row >= col, 0.0, MASKED)

    @pl.when(pl.program_id(1) == 0)
    def _():
        halo[...] = jnp.zeros_like(halo)
        state[...] = jnp.zeros_like(state)
        sh0 = DELTA_BASE_BLOCK.bit_length() - 1
        msk_s[0:c, :] = jnp.where(((row >> sh0) == (col >> sh0)) & (row > col), -1.0, 0.0)
        rr = lax.broadcasted_iota(jnp.int32, (half, c), 0)
        cc = lax.broadcasted_iota(jnp.int32, (half, c), 1)
        for lvl, blk in enumerate(levels):
            sh = blk.bit_length() - 1
            msk_s[c + lvl * half:c + (lvl + 1) * half, :] = ((cc >> sh) == 2 * (rr >> sh)).astype(F32)

    beta = jax.nn.sigmoid(ba_ref[:, 0:LANES])
    a_in = ba_ref[:, LANES:2 * LANES] + dtb_ref[...]
    softplus = jnp.maximum(a_in, 0.0) + jnp.log1p(jnp.exp(-jnp.abs(a_in)))
    g = -jnp.exp(alog_ref[...]) * softplus
    gc = jnp.dot((row >= col).astype(F32), g, precision=lax.Precision.HIGHEST, preferred_element_type=F32)
    eg = jnp.exp(gc)
    egl = jnp.exp(gc[c - 1:c, :] - gc)
    gct = gc.T
    eg_last = jnp.exp(jnp.broadcast_to(gct[:, c - 1:c], (LANES, LANES)))
    bscale = beta * eg
    g2 = gc * LOG2E
    g2t = gct * LOG2E

    for s0 in range(0, qkv_ref.shape[1], strip):
        cs = slice(s0, s0 + strip)
        x = qkv_ref[:, cs].astype(F32)
        ext[0:hl, cs] = halo[:, cs]
        ext[hl:hl + c, cs] = x
        halo[:, cs] = x[c - hl:, :]

    def conv_act(c0):
        cs = slice(c0, c0 + dk)
        cur = ext[hl:hl + c, cs]
        prev = ext[0:c, cs]
        half_w = 0.5 * cw_ref[:, cs]
        y = cur * half_w[DN_CONV_WIDTH - 1:DN_CONV_WIDTH]
        for j in range(1, DN_CONV_WIDTH):
            y = y + _shift_rows(cur, prev, j) * half_w[DN_CONV_WIDTH - 1 - j:DN_CONV_WIDTH - j]
        return _silu_of_half(y)

    def l2n(a, scale=1.0):
        return a * (lax.rsqrt(jnp.sum(a * a, axis=-1, keepdims=True) + EPS) * scale)

    def dot(a, b):
        return jnp.dot(a, b, preferred_element_type=F32)

    ng = ng_ref[...]
    nt = (((1,), (1,)), ((), ()))
    for j0 in range(0, n_qk, DELTA_GROUP_QK):
        js = list(range(j0, min(j0 + DELTA_GROUP_QK, n_qk)))
        hvs = [rep * j + r for j in js for r in range(rep)]
        q = {j: l2n(conv_act(j * dk), dk ** -0.5) for j in js}
        k = {j: l2n(conv_act(kw + j * dk)) for j in js}
        kb = {j: k[j].astype(BF16) for j in js}
        p = {j: lax.dot_general(jnp.concatenate([q[j].astype(BF16), kb[j]], axis=0), kb[j], nt,
                                preferred_element_type=F32) for j in js}
        dm, a_raw = {}, {}
        for hv in hvs:
            dm[hv] = jnp.exp2(g2[:, hv:hv + 1] - g2t[hv:hv + 1, :] + not_causal)
            a_raw[hv] = p[hv // rep][c:2 * c] * (dm[hv] * beta[:, hv:hv + 1])
        n1 = {hv: a_raw[hv] * msk_s[0:c, :] for hv in hvs}
        t = {hv: eye + n1[hv] for hv in hvs}
        pw = {hv: n1[hv].astype(BF16) for hv in hvs}
        sq = 1
        while 2 * sq < DELTA_BASE_BLOCK:
            pw = {hv: dot(pw[hv], pw[hv]).astype(BF16) for hv in hvs}
            t = {hv: t[hv] + dot(t[hv].astype(BF16), pw[hv]) for hv in hvs}
            sq *= 2
        for lvl, blk in enumerate(levels):
            lm = msk_s[c + lvl * half:c + (lvl + 1) * half, :]
            m = {hv: dot((_odd_blocks(a_raw[hv], blk) * lm).astype(BF16), t[hv].astype(BF16)) for hv in hvs}
            t = {hv: _odd_merge(t[hv], _odd_blocks(t[hv], blk)
                                - dot(_odd_blocks(t[hv], blk).astype(BF16), _odd_scatter(m[hv], blk).astype(BF16)),
                                blk) for hv in hvs}
        uw = {}
        for hv in hvs:
            v = conv_act(2 * kw + hv * dk)
            rhs = jnp.concatenate([(v * beta[:, hv:hv + 1]).astype(BF16),
                                   (k[hv // rep] * bscale[:, hv:hv + 1]).astype(BF16)], axis=1)
            uw[hv] = dot(t[hv].astype(BF16), rhs)
        st = {hv: state[hv] for hv in hvs}
        wq = {hv: dot(jnp.concatenate([uw[hv][:, dk:2 * dk].astype(BF16),
                                       (q[hv // rep] * eg[:, hv:hv + 1]).astype(BF16)], axis=0),
                      st[hv].astype(BF16)) for hv in hvs}
        v_new = {hv: (uw[hv][:, 0:dk] - wq[hv][0:c]).astype(BF16) for hv in hvs}
        o = {hv: wq[hv][c:2 * c] + dot((p[hv // rep][0:c] * dm[hv]).astype(BF16), v_new[hv]) for hv in hvs}
        for hv in hvs:
            kdt = (k[hv // rep] * egl[:, hv:hv + 1]).T.astype(BF16)
            state[hv] = st[hv] * eg_last[hv:hv + 1, :] + dot(kdt, v_new[hv])
        for hv in hvs:
            cols = slice(hv * dk, (hv + 1) * dk)
            on = o[hv] * lax.rsqrt(jnp.mean(o[hv] * o[hv], axis=-1, keepdims=True) + EPS) * ng
            o_ref[:, cols] = (on * _silu(z_ref[:, cols].astype(F32))).astype(o_ref.dtype)


def _delta_mix(proj, ba, conv_w, a_log, dt_bias, norm_g, batch, seq, bw):
    n = proj.shape[0]
    c = min(DELTA_CHUNK, seq)
    dk = DN_HEAD_DIM
    n_v = bw // dk
    n_qk = n_v // 2
    cw = 2 * n_qk * dk + bw
    nch = seq // c
    strip = 512
    n_lvl = (c // DELTA_BASE_BLOCK).bit_length() - 1
    kern = functools.partial(_delta_kernel, c=c, n_qk=n_qk, strip=strip)

    def pad_lanes(v):
        return jnp.zeros((1, LANES), F32).at[0, :v.shape[0]].set(v.astype(F32))

    rowblk = lambda b, t: (b * nch + t, 0)
    est = (2 * (c * cw * 2 + 2 * c * bw * 2 + c * 2 * LANES * 4) + (2 * SUBLANES + c) * cw * 4
           + n_v * dk * dk * 4 + (2 + n_lvl) * c * c * 2 + 16 * 1024 * 1024)
    return pl.pallas_call(
        kern,
        grid=(batch, nch),
        in_specs=[
            pl.BlockSpec((c, cw), rowblk),
            pl.BlockSpec((c, bw), lambda b, t: (b * nch + t, cw // bw)),
            pl.BlockSpec((c, 2 * LANES), rowblk),
            pl.BlockSpec((DN_CONV_WIDTH, cw), lambda b, t: (0, 0)),
            pl.BlockSpec((1, LANES), lambda b, t: (0, 0)),
            pl.BlockSpec((1, LANES), lambda b, t: (0, 0)),
            pl.BlockSpec((1, dk), lambda b, t: (0, 0)),
        ],
        out_specs=pl.BlockSpec((c, bw), rowblk),
        out_shape=jax.ShapeDtypeStruct((n, bw), BF16),
        scratch_shapes=[
            pltpu.VMEM((SUBLANES, cw), F32),
            pltpu.VMEM((SUBLANES + c, cw), F32),
            pltpu.VMEM((n_v, dk, dk), F32),
            pltpu.VMEM((c + n_lvl * (c // 2), c), F32),
        ],
        compiler_params=pltpu.CompilerParams(dimension_semantics=("arbitrary", "arbitrary"),
                                             vmem_limit_bytes=_vmem_limit(est)),
        name="delta_mix",
    )(proj, proj, ba, conv_w.astype(F32), pad_lanes(a_log), pad_lanes(dt_bias),
      norm_g.reshape(1, dk).astype(F32))


def kernel(x, mem, layer_norm_g, mem_norm_g, final_norm_g, w_in_pool, pool_maps, pool_scale, w_in_delta,
           dn_conv_w, dn_a_log, dn_dt_bias, dn_norm_g, w_mem_kv, w_out):
    batch, seq, d = x.shape
    mem_tokens = mem.shape[1]
    depth = layer_norm_g.shape[0]
    n = batch * seq
    bw = pool_scale.shape[1]
    xw = w_mem_kv.shape[2] // 2
    n_v = dn_a_log.shape[1]
    main_w = w_in_delta.shape[2] - 2 * n_v
    cwid = main_w - (bw + xw) - xw

    w_in_pool_b = w_in_pool.astype(BF16)
    w_in_delta_b = w_in_delta.astype(BF16)
    pool_maps_b = pool_maps.astype(BF16)
    w_mem_kv_b = w_mem_kv.astype(BF16)
    w_out_b = w_out.astype(BF16)

    mem_n = _rmsnorm(mem.reshape(batch * mem_tokens, d), mem_norm_g, BF16)
    kv = _matmul(mem_n, w_mem_kv_b, depth, 2 * xw, BF16, name="mem_kv")

    h = x.reshape(n, d)
    xn = _rmsnorm(h, layer_norm_g[0], BF16)
    for layer in range(depth):
        j = layer // 2
        if layer % 2 == 0:
            proj = _matmul(xn, w_in_pool_b, 1, w_in_pool.shape[2], BF16, layer0=j, name="in_proj_pool")
            branch = _pool_mix(proj, pool_maps_b, j, pool_scale[j], seq, bw)
            z_mem_blk, q_blk = (bw + bw) // xw, (bw + bw + xw) // xw
        else:
            side = jnp.zeros((d, 2 * LANES), BF16)
            side = side.at[:, 0:n_v].set(w_in_delta[j, :, main_w:main_w + n_v].astype(BF16))
            side = side.at[:, LANES:LANES + n_v].set(w_in_delta[j, :, main_w + n_v:].astype(BF16))
            proj, ba = _matmul(xn, w_in_delta_b, 1, main_w, BF16, layer0=j, side_w=side, name="in_proj_delta")
            branch = _delta_mix(proj, ba, dn_conv_w[j], dn_a_log[j], dn_dt_bias[j], dn_norm_g[j], batch, seq, bw)
            z_mem_blk, q_blk = (cwid + bw) // xw, (cwid + bw + xw) // xw
        final = layer == depth - 1
        g_next = final_norm_g if final else layer_norm_g[layer + 1]
        outs = _mem_out(branch, proj, kv, w_out_b, layer, h, g_next, final, seq, mem_tokens, q_blk, z_mem_blk, xw)
        if final:
            return outs[0].reshape(batch, seq, d)
        h, xn = outs
```

```python
import functools

import jax
import jax.numpy as jnp
from jax import lax
from jax.experimental import pallas as pl
from jax.experimental.pallas import tpu as pltpu

F32 = jnp.float32
BF16 = jnp.bfloat16
EPS = 1e-6
LOG2E = 1.4426950408889634

POOL_WINDOWS = (2, 4, 8, 16)
POOL_HALO = 32
XA_HEADS = 4
DN_HEAD_DIM = 128
DN_CONV_WIDTH = 4
DELTA_CHUNK = 128
DELTA_GROUP_QK = 8
DELTA_BASE_BLOCK = 8
SUBLANES = 8
LANES = 128
MASKED = -1e30

V7X_VMEM_BYTES = 64 * 1024 * 1024


def _vmem_limit(estimate_bytes):
    return int(min(max(estimate_bytes * 5 // 4, 16 * 1024 * 1024), V7X_VMEM_BYTES - 6 * 1024 * 1024))


def _silu_of_half(h):
    return h + h * jnp.tanh(h)


def _silu(x):
    return _silu_of_half(0.5 * x)


def _shift_rows(cur, prev, s):
    r, w = cur.shape
    tiles = (r // SUBLANES, SUBLANES, w)
    wraps = lax.broadcasted_iota(jnp.int32, (1, SUBLANES, w), 1) >= SUBLANES - s
    m = jnp.where(wraps, prev.reshape(tiles), cur.reshape(tiles))
    return pltpu.roll(m, s, axis=1).reshape(r, w)


def _rmsnorm_kernel(x_ref, g_ref, o_ref):
    x = x_ref[...].astype(F32)
    ms = jnp.mean(x * x, axis=-1, keepdims=True)
    o_ref[...] = (x * lax.rsqrt(ms + EPS) * g_ref[...]).astype(o_ref.dtype)


def _rmsnorm(x2d, g, out_dtype, tm=512):
    n, d = x2d.shape
    tm = min(tm, n)
    return pl.pallas_call(
        _rmsnorm_kernel,
        grid=(n // tm,),
        in_specs=[pl.BlockSpec((tm, d), lambda i: (i, 0)), pl.BlockSpec((1, d), lambda i: (0, 0))],
        out_specs=pl.BlockSpec((tm, d), lambda i: (i, 0)),
        out_shape=jax.ShapeDtypeStruct((n, d), out_dtype),
        compiler_params=pltpu.CompilerParams(dimension_semantics=("arbitrary",)),
        name="rmsnorm_cast",
    )(x2d, g.reshape(1, d).astype(F32))


def _mm_kernel(a_ref, w_ref, o_ref):
    o_ref[...] = jnp.dot(a_ref[...], w_ref[...], preferred_element_type=F32).astype(o_ref.dtype)


def _mm_side_kernel(a_ref, w_ref, ws_ref, o_ref, os_ref):
    o_ref[...] = jnp.dot(a_ref[...], w_ref[...], preferred_element_type=F32).astype(o_ref.dtype)

    @pl.when(pl.program_id(1) == 0)
    def _():
        os_ref[...] = jnp.dot(a_ref[...], ws_ref[...], preferred_element_type=F32)


def _matmul(a, w3, n_layers, n_out, out_dtype, layer0=0, side_w=None, tm=1024, tn=2048, name="proj"):
    m, k = a.shape
    tm, tn = min(tm, m), min(tn, n_out)
    npl = n_out // tn
    grid = (m // tm, n_layers * npl)
    a_spec = pl.BlockSpec((tm, k), lambda i, j: (i, 0))
    w_spec = pl.BlockSpec((None, k, tn), lambda i, j: (layer0 + j // npl, 0, j % npl))
    o_spec = pl.BlockSpec((tm, tn), lambda i, j: (i, j))
    n_total = n_layers * n_out
    est = 2 * (tm * k * 2 + k * tn * 2 + tm * tn * 2) + tm * tn * 4
    params = pltpu.CompilerParams(dimension_semantics=("arbitrary", "arbitrary"),
                                  vmem_limit_bytes=_vmem_limit(est + 4 * 1024 * 1024))
    if side_w is None:
        return pl.pallas_call(
            _mm_kernel, grid=grid, in_specs=[a_spec, w_spec], out_specs=o_spec,
            out_shape=jax.ShapeDtypeStruct((m, n_total), out_dtype), compiler_params=params, name=name,
        )(a, w3)
    ns = side_w.shape[1]
    return pl.pallas_call(
        _mm_side_kernel, grid=grid,
        in_specs=[a_spec, w_spec, pl.BlockSpec((k, ns), lambda i, j: (0, 0))],
        out_specs=[o_spec, pl.BlockSpec((tm, ns), lambda i, j: (i, 0))],
        out_shape=[jax.ShapeDtypeStruct((m, n_total), out_dtype), jax.ShapeDtypeStruct((m, ns), F32)],
        compiler_params=params, name=name,
    )(a, w3, side_w)


def _pool_kernel(u_ref, z_ref, maps_ref, scale_ref, o_ref, buf_a, buf_b, halo, *, tm, tiles_per_seq, cg):
    t = pl.program_id(0) % tiles_per_seq
    hl = POOL_HALO
    ext = hl + tm
    pos = (t * tm + 1 + lax.broadcasted_iota(jnp.int32, (tm, 1), 0)).astype(F32)

    @pl.when(t == 0)
    def _():
        halo[...] = jnp.zeros_like(halo)

    for g, w in enumerate(POOL_WINDOWS):
        cols = slice(g * cg, (g + 1) * cg)
        u = u_ref[:, cols].astype(F32)
        buf_a[0:hl, :] = halo[:, cols]
        buf_a[hl:ext, :] = u
        halo[:, cols] = u[tm - hl:, :]
        src, dst, lo, shift = buf_a, buf_b, 0, 1
        while shift < w:
            lo += SUBLANES
            cur = src[lo:ext, :]
            prev = src[lo - SUBLANES:ext - SUBLANES, :]
            dst[lo:ext, :] = cur + (_shift_rows(cur, prev, shift) if shift < SUBLANES else prev)
            src, dst, shift = dst, src, shift * 2
        inv_cnt = 1.0 / jnp.minimum(pos, float(w))
        d = (src[hl:ext, :] * inv_cnt - u).astype(BF16)
        y = jnp.dot(d, maps_ref[g], preferred_element_type=F32)
        o_ref[:, cols] = (y * scale_ref[:, cols] * _silu(z_ref[:, cols].astype(F32))).astype(o_ref.dtype)


def _pool_mix(proj, maps4, layer, scale, seq, bw, tm=256):
    n = proj.shape[0]
    tm = min(tm, seq)
    ng, cg = maps4.shape[1], maps4.shape[2]
    kern = functools.partial(_pool_kernel, tm=tm, tiles_per_seq=seq // tm, cg=cg)
    est = 2 * (3 * tm * bw * 2 + ng * cg * cg * 2) + 2 * (POOL_HALO + tm) * cg * 4 + POOL_HALO * bw * 4 + 4 * tm * cg * 4
    return pl.pallas_call(
        kern,
        grid=(n // tm,),
        in_specs=[
            pl.BlockSpec((tm, bw), lambda i: (i, 0)),
            pl.BlockSpec((tm, bw), lambda i: (i, 1)),
            pl.BlockSpec((None, ng, cg, cg), lambda i: (layer, 0, 0, 0)),
            pl.BlockSpec((1, bw), lambda i: (0, 0)),
        ],
        out_specs=pl.BlockSpec((tm, bw), lambda i: (i, 0)),
        out_shape=jax.ShapeDtypeStruct((n, bw), BF16),
        scratch_shapes=[
            pltpu.VMEM((POOL_HALO + tm, cg), F32),
            pltpu.VMEM((POOL_HALO + tm, cg), F32),
            pltpu.VMEM((POOL_HALO, bw), F32),
        ],
        compiler_params=pltpu.CompilerParams(dimension_semantics=("arbitrary",), vmem_limit_bytes=_vmem_limit(est)),
        name="pool_mix",
    )(proj, proj, maps4, scale.reshape(1, bw).astype(F32))


def _mem_out_kernel(br_ref, q_ref, z_ref, k_ref, v_ref, w_ref, h_ref, g_ref, *out_refs, bw, hd, final):
    c = hd ** -0.5 * LOG2E
    heads = [slice(h * hd, (h + 1) * hd) for h in range(XA_HEADS)]
    s = [lax.dot_general(q_ref[:, cols], k_ref[:, cols], (((1,), (1,)), ((), ())), preferred_element_type=F32)
         for cols in heads]
    kc = bw // XA_HEADS
    acc = None
    p = []
    for i, sh in enumerate(s):
        part = jnp.dot(br_ref[:, i * kc:(i + 1) * kc], w_ref[i * kc:(i + 1) * kc, :], preferred_element_type=F32)
        acc = part if acc is None else acc + part
        e = jnp.exp2((sh - jnp.max(sh, axis=-1, keepdims=True)) * c)
        p.append((e * (1.0 / jnp.sum(e, axis=-1, keepdims=True))).astype(BF16))
    o = [jnp.dot(ph, v_ref[:, cols], preferred_element_type=F32) for ph, cols in zip(p, heads)]
    memo = jnp.concatenate([(oh * _silu(z_ref[:, cols].astype(F32))).astype(BF16) for oh, cols in zip(o, heads)],
                           axis=1)
    acc = acc + jnp.dot(memo, w_ref[bw:, :], preferred_element_type=F32)
    hn = h_ref[...] + acc
    y = hn * lax.rsqrt(jnp.mean(hn * hn, axis=-1, keepdims=True) + EPS) * g_ref[...]
    if final:
        out_refs[0][...] = y
    else:
        out_refs[0][...] = hn
        out_refs[1][...] = y.astype(out_refs[1].dtype)


def _mem_out(branch, proj, kv, w_out3, layer, h, g_next, final, seq, mem_tokens, q_blk, z_blk, xw, tm=256):
    n, bw = branch.shape
    mw, d = w_out3.shape[1], w_out3.shape[2]
    tm = min(tm, seq)
    tps = seq // tm
    kern = functools.partial(_mem_out_kernel, bw=bw, hd=xw // XA_HEADS, final=final)
    row = lambda i: (i, 0)
    if final:
        out_specs = [pl.BlockSpec((tm, d), row)]
        out_shape = [jax.ShapeDtypeStruct((n, d), F32)]
    else:
        out_specs = [pl.BlockSpec((tm, d), row), pl.BlockSpec((tm, d), row)]
        out_shape = [jax.ShapeDtypeStruct((n, d), F32), jax.ShapeDtypeStruct((n, d), BF16)]
    est = (mw * d * 2 + 2 * (tm * (bw + 2 * xw) * 2 + 2 * mem_tokens * xw * 2 + tm * d * 4 * 2 + tm * d * 2)
           + 3 * tm * d * 4 + 6 * tm * mem_tokens * 4)
    return pl.pallas_call(
        kern,
        grid=(n // tm,),
        in_specs=[
            pl.BlockSpec((tm, bw), row),
            pl.BlockSpec((tm, xw), lambda i: (i, q_blk)),
            pl.BlockSpec((tm, xw), lambda i: (i, z_blk)),
            pl.BlockSpec((mem_tokens, xw), lambda i: (i // tps, 2 * layer)),
            pl.BlockSpec((mem_tokens, xw), lambda i: (i // tps, 2 * layer + 1)),
            pl.BlockSpec((None, mw, d), lambda i: (layer, 0, 0), pipeline_mode=pl.Buffered(1)),
            pl.BlockSpec((tm, d), row),
            pl.BlockSpec((1, d), lambda i: (0, 0)),
        ],
        out_specs=out_specs,
        out_shape=out_shape,
        compiler_params=pltpu.CompilerParams(dimension_semantics=("arbitrary",), vmem_limit_bytes=_vmem_limit(est)),
        name="mem_out_proj",
    )(branch, proj, proj, kv, kv, w_out3, h, g_next.reshape(1, d).astype(F32))


def _odd_blocks(x, b):
    return jnp.concatenate([x[i:i + b] for i in range(b, x.shape[0], 2 * b)], axis=0)


def _odd_scatter(y, b):
    zero = jnp.zeros((b, y.shape[1]), y.dtype)
    parts = []
    for i in range(0, y.shape[0], b):
        parts += [zero, y[i:i + b]]
    return jnp.concatenate(parts, axis=0)


def _odd_merge(x, y, b):
    parts = []
    for m, i in enumerate(range(0, x.shape[0], 2 * b)):
        parts += [x[i:i + b], y[m * b:(m + 1) * b]]
    return jnp.concatenate(parts, axis=0)


def _delta_kernel(qkv_ref, z_ref, ba_ref, cw_ref, alog_ref, dtb_ref, ng_ref, o_ref,
                  halo, ext, state, msk_s, *, c, n_qk, strip):
    dk = DN_HEAD_DIM
    rep = 2
    kw = n_qk * dk
    hl = SUBLANES
    half = c // 2
    levels = []
    b = DELTA_BASE_BLOCK
    while b < c:
        levels.append(b)
        b *= 2
    row = lax.broadcasted_iota(jnp.int32, (c, c), 0)
    col = lax.broadcasted_iota(jnp.int32, (c, c), 1)
    eye = (row == col).astype(F32)
    not_causal = jnp.where(row >= col, 0.0, MASKED)

    @pl.when(pl.program_id(1) == 0)
    def _():
        halo[...] = jnp.zeros_like(halo)
        state[...] = jnp.zeros_like(state)
        sh0 = DELTA_BASE_BLOCK.bit_length() - 1
        msk_s[0:c, :] = jnp.where(((row >> sh0) == (col >> sh0)) & (row > col), -1.0, 0.0)
        rr = lax.broadcasted_iota(jnp.int32, (half, c), 0)
        cc = lax.broadcasted_iota(jnp.int32, (half, c), 1)
        for lvl, blk in enumerate(levels):
            sh = blk.bit_length() - 1
            msk_s[c + lvl * half:c + (lvl + 1) * half, :] = ((cc >> sh) == 2 * (rr >> sh)).astype(F32)

    beta = jax.nn.sigmoid(ba_ref[:, 0:LANES])
    a_in = ba_ref[:, LANES:2 * LANES] + dtb_ref[...]
    softplus = jnp.maximum(a_in, 0.0) + jnp.log1p(jnp.exp(-jnp.abs(a_in)))
    g = -jnp.exp(alog_ref[...]) * softplus
    gc = jnp.dot((row >= col).astype(F32), g, precision=lax.Precision.HIGHEST, preferred_element_type=F32)
    eg = jnp.exp(gc)
    egl = jnp.exp(gc[c - 1:c, :] - gc)
    gct = gc.T
    eg_last = jnp.exp(jnp.broadcast_to(gct[:, c - 1:c], (LANES, LANES)))
    bscale = beta * eg
    g2 = gc * LOG2E
    g2t = gct * LOG2E

    for s0 in range(0, qkv_ref.shape[1], strip):
        cs = slice(s0, s0 + strip)
        x = qkv_ref[:, cs].astype(F32)
        ext[0:hl, cs] = halo[:, cs]
        ext[hl:hl + c, cs] = x
        halo[:, cs] = x[c - hl:, :]

    def conv_act(c0):
        cs = slice(c0, c0 + dk)
        cur = ext[hl:hl + c, cs]
        prev = ext[0:c, cs]
        half_w = 0.5 * cw_ref[:, cs]
        y = cur * half_w[DN_CONV_WIDTH - 1:DN_CONV_WIDTH]
        for j in range(1, DN_CONV_WIDTH):
            y = y + _shift_rows(cur, prev, j) * half_w[DN_CONV_WIDTH - 1 - j:DN_CONV_WIDTH - j]
        return _silu_of_half(y)

    def l2n(a, scale=1.0):
        return a * (lax.rsqrt(jnp.sum(a * a, axis=-1, keepdims=True) + EPS) * scale)

    def dot(a, b):
        return jnp.dot(a, b, preferred_element_type=F32)

    ng = ng_ref[...]
    nt = (((1,), (1,)), ((), ()))
    for j0 in range(0, n_qk, DELTA_GROUP_QK):
        js = list(range(j0, min(j0 + DELTA_GROUP_QK, n_qk)))
        hvs = [rep * j + r for j in js for r in range(rep)]
        q = {j: l2n(conv_act(j * dk), dk ** -0.5) for j in js}
        k = {j: l2n(conv_act(kw + j * dk)) for j in js}
        kb = {j: k[j].astype(BF16) for j in js}
        p = {j: lax.dot_general(jnp.concatenate([q[j].astype(BF16), kb[j]], axis=0), kb[j], nt,
                                preferred_element_type=F32) for j in js}
        dm, a_raw = {}, {}
        for hv in hvs:
            dm[hv] = jnp.exp2(g2[:, hv:hv + 1] - g2t[hv:hv + 1, :] + not_causal)
            a_raw[hv] = p[hv // rep][c:2 * c] * (dm[hv] * beta[:, hv:hv + 1])
        n1 = {hv: a_raw[hv] * msk_s[0:c, :] for hv in hvs}
        t = {hv: eye + n1[hv] for hv in hvs}
        pw = {hv: n1[hv].astype(BF16) for hv in hvs}
        sq = 1
        while 2 * sq < DELTA_BASE_BLOCK:
            pw = {hv: dot(pw[hv], pw[hv]).astype(BF16) for hv in hvs}
            t = {hv: t[hv] + dot(t[hv].astype(BF16), pw[hv]) for hv in hvs}
            sq *= 2
        for lvl, blk in enumerate(levels):
            lm = msk_s[c + lvl * half:c + (lvl + 1) * half, :]
            m = {hv: dot((_odd_blocks(a_raw[hv], blk) * lm).astype(BF16), t[hv].astype(BF16)) for hv in hvs}
            t = {hv: _odd_merge(t[hv], _odd_blocks(t[hv], blk)
                                - dot(_odd_blocks(t[hv], blk).astype(BF16), _odd_scatter(m[hv], blk).astype(BF16)),
                                blk) for hv in hvs}
        uw = {}
        for hv in hvs:
            v = conv_act(2 * kw + hv * dk)
            rhs = jnp.concatenate([(v * beta[:, hv:hv + 1]).astype(BF16),
                                   (k[hv // rep] * bscale[:, hv:hv + 1]).astype(BF16)], axis=1)
            uw[hv] = dot(t[hv].astype(BF16), rhs)
        st = {hv: state[hv] for hv in hvs}
        wq = {hv: dot(jnp.concatenate([uw[hv][:, dk:2 * dk].astype(BF16),
                                       (q[hv // rep] * eg[:, hv:hv + 1]).astype(BF16)], axis=0),
                      st[hv].astype(BF16)) for hv in hvs}
        v_new = {hv: (uw[hv][:, 0:dk] - wq[hv][0:c]).astype(BF16) for hv in hvs}
        o = {hv: wq[hv][c:2 * c] + dot((p[hv // rep][0:c] * dm[hv]).astype(BF16), v_new[hv]) for hv in hvs}
        for hv in hvs:
            kdt = (k[hv // rep] * egl[:, hv:hv + 1]).T.astype(BF16)
            state[hv] = st[hv] * eg_last[hv:hv + 1, :] + dot(kdt, v_new[hv])
        for hv in hvs:
            cols = slice(hv * dk, (hv + 1) * dk)
            on = o[hv] * lax.rsqrt(jnp.mean(o[hv] * o[hv], axis=-1, keepdims=True) + EPS) * ng
            o_ref[:, cols] = (on * _silu(z_ref[:, cols].astype(F32))).astype(o_ref.dtype)


def _delta_mix(proj, ba, conv_w, a_log, dt_bias, norm_g, batch, seq, bw):
    n = proj.shape[0]
    c = min(DELTA_CHUNK, seq)
    dk = DN_HEAD_DIM
    n_v = bw // dk
    n_qk = n_v // 2
    cw = 2 * n_qk * dk + bw
    nch = seq // c
    strip = 512
    n_lvl = (c // DELTA_BASE_BLOCK).bit_length() - 1
    kern = functools.partial(_delta_kernel, c=c, n_qk=n_qk, strip=strip)

    def pad_lanes(v):
        return jnp.zeros((1, LANES), F32).at[0, :v.shape[0]].set(v.astype(F32))

    rowblk = lambda b, t: (b * nch + t, 0)
    est = (2 * (c * cw * 2 + 2 * c * bw * 2 + c * 2 * LANES * 4) + (2 * SUBLANES + c) * cw * 4
           + n_v * dk * dk * 4 + (2 + n_lvl) * c * c * 2 + 16 * 1024 * 1024)
    return pl.pallas_call(
        kern,
        grid=(batch, nch),
        in_specs=[
            pl.BlockSpec((c, cw), rowblk),
            pl.BlockSpec((c, bw), lambda b, t: (b * nch + t, cw // bw)),
            pl.BlockSpec((c, 2 * LANES), rowblk),
            pl.BlockSpec((DN_CONV_WIDTH, cw), lambda b, t: (0, 0)),
            pl.BlockSpec((1, LANES), lambda b, t: (0, 0)),
            pl.BlockSpec((1, LANES), lambda b, t: (0, 0)),
            pl.BlockSpec((1, dk), lambda b, t: (0, 0)),
        ],
        out_specs=pl.BlockSpec((c, bw), rowblk),
        out_shape=jax.ShapeDtypeStruct((n, bw), BF16),
        scratch_shapes=[
            pltpu.VMEM((SUBLANES, cw), F32),
            pltpu.VMEM((SUBLANES + c, cw), F32),
            pltpu.VMEM((n_v, dk, dk), F32),
            pltpu.VMEM((c + n_lvl * (c // 2), c), F32),
        ],
        compiler_params=pltpu.CompilerParams(dimension_semantics=("arbitrary", "arbitrary"),
                                             vmem_limit_bytes=_vmem_limit(est)),
        name="delta_mix",
    )(proj, proj, ba, conv_w.astype(F32), pad_lanes(a_log), pad_lanes(dt_bias),
      norm_g.reshape(1, dk).astype(F32))


def kernel(x, mem, layer_norm_g, mem_norm_g, final_norm_g, w_in_pool, pool_maps, pool_scale, w_in_delta,
           dn_conv_w, dn_a_log, dn_dt_bias, dn_norm_g, w_mem_kv, w_out):
    batch, seq, d = x.shape
    mem_tokens = mem.shape[1]
    depth = layer_norm_g.shape[0]
    n = batch * seq
    bw = pool_scale.shape[1]
    xw = w_mem_kv.shape[2] // 2
    n_v = dn_a_log.shape[1]
    main_w = w_in_delta.shape[2] - 2 * n_v
    cwid = main_w - (bw + xw) - xw

    w_in_pool_b = w_in_pool.astype(BF16)
    w_in_delta_b = w_in_delta.astype(BF16)
    pool_maps_b = pool_maps.astype(BF16)
    w_mem_kv_b = w_mem_kv.astype(BF16)
    w_out_b = w_out.astype(BF16)

    mem_n = _rmsnorm(mem.reshape(batch * mem_tokens, d), mem_norm_g, BF16)
    kv = _matmul(mem_n, w_mem_kv_b, depth, 2 * xw, BF16, name="mem_kv")

    h = x.reshape(n, d)
    xn = _rmsnorm(h, layer_norm_g[0], BF16)
    for layer in range(depth):
        j = layer // 2
        if layer % 2 == 0:
            proj = _matmul(xn, w_in_pool_b, 1, w_in_pool.shape[2], BF16, layer0=j, name="in_proj_pool")
            branch = _pool_mix(proj, pool_maps_b, j, pool_scale[j], seq, bw)
            z_mem_blk, q_blk = (bw + bw) // xw, (bw + bw + xw) // xw
        else:
            side = jnp.zeros((d, 2 * LANES), BF16)
            side = side.at[:, 0:n_v].set(w_in_delta[j, :, main_w:main_w + n_v].astype(BF16))
            side = side.at[:, LANES:LANES + n_v].set(w_in_delta[j, :, main_w + n_v:].astype(BF16))
            proj, ba = _matmul(xn, w_in_delta_b, 1, main_w, BF16, layer0=j, side_w=side, name="in_proj_delta")
            branch = _delta_mix(proj, ba, dn_conv_w[j], dn_a_log[j], dn_dt_bias[j], dn_norm_g[j], batch, seq, bw)
            z_mem_blk, q_blk = (cwid + bw) // xw, (cwid + bw + xw) // xw
        final = layer == depth - 1
        g_next = final_norm_g if final else layer_norm_g[layer + 1]
        outs = _mem_out(branch, proj, kv, w_out_b, layer, h, g_next, final, seq, mem_tokens, q_blk, z_mem_blk, xw)
        if final:
            return outs[0].reshape(batch, seq, d)
        h, xn = outs
```
